```python
import math
import jax
import jax.numpy as jnp
from jax import lax
import numpy as np

D_MODEL = 2048
BATCH = 1
SEQ = 8192
DEPTH = 2
DEC_BATCH = 128
DEC_SEQ = 8
PAST_LEN = 8192
PAGE_SIZE = 128

N_A_LAYERS = DEPTH // 2
N_B_LAYERS = DEPTH - N_A_LAYERS
N_HEADS = 16
HEAD_DIM = D_MODEL // N_HEADS
SCALE = HEAD_DIM ** -0.5
KV_A = 2
G_A = N_HEADS // KV_A
CMP_LEN = 32
CMP_STRIDE = 16
SEL_BLOCK = 64
SEL_TOPK = 16
N_LOCAL_FORCED = 2
WIN_A = 512
N_BRANCH = 3
FORCED_SCORE = 1e9
NSA_IN_DIM = N_HEADS * HEAD_DIM + 6 * KV_A * HEAD_DIM + N_BRANCH * N_HEADS
KV_B = 2
G_B = N_HEADS // KV_B
WIN_B = 128
NUM_BUCKETS = 32
MAX_EXACT = 16
MAX_DISTANCE = 128
PEER_HEADS = 8
N_KEYS = 128
N_EXPERTS = N_KEYS * N_KEYS
PEER_KEY_DIM = 256
PEER_HALF = PEER_KEY_DIM // 2
PEER_TOPK = 16
Q_BLOCK = 128
TOK_BLOCK = 128
EPS = 1e-6

kernel_name = 'yoco_nsa_swa_peer_step'


def block_len(n):
    return Q_BLOCK if n % Q_BLOCK == 0 else n


def rms_norm(x, gain):
    xf = x.astype(jnp.float32)
    y = xf * lax.rsqrt(jnp.mean(xf * xf, axis=-1, keepdims=True) + EPS)
    return (y * gain.astype(jnp.float32)).astype(x.dtype)


def modulate(h, gain, shift, scale):
    return rms_norm(h, gain) * (1.0 + scale[:, None, :]) + shift[:, None, :]


def t5_bucket(dist):
    n = jnp.maximum(dist, 0)
    nf = jnp.maximum(n, 1).astype(jnp.float32)
    large = MAX_EXACT + (jnp.log(nf / MAX_EXACT) / math.log(MAX_DISTANCE / MAX_EXACT) * (NUM_BUCKETS - MAX_EXACT)).astype(jnp.int32)
    large = jnp.minimum(large, NUM_BUCKETS - 1)
    return jnp.where(n < MAX_EXACT, n, large)


def masked_softmax(logits, mask):
    logits = jnp.where(mask, logits.astype(jnp.float32), -jnp.inf)
    m = jnp.max(logits, axis=-1, keepdims=True)
    m = jnp.where(jnp.isfinite(m), m, 0.0)
    e = jnp.exp(logits - m)
    s = jnp.sum(e, axis=-1, keepdims=True)
    return e / jnp.where(s > 0, s, 1.0)


def window_block(q, t, k, v, s, rel_bias, window, sinks):
    qb, kv, g, _ = q.shape
    nk = k.shape[0]
    dist = t[:, None] - s[None, :]
    mask = (dist >= 0) & (dist <= window) & (s[None, :] >= 0)
    bias = jnp.transpose(rel_bias[t5_bucket(dist)].reshape(qb, nk, kv, g), (0, 2, 3, 1))
    logits = jnp.einsum('qkgd,skd->qkgs', q, k).astype(jnp.float32) * SCALE + bias
    mask = jnp.broadcast_to(mask[:, None, None, :], logits.shape)
    if sinks is not None:
        sink = jnp.broadcast_to(sinks.astype(jnp.float32).reshape(1, kv, g, 1), (qb, kv, g, 1))
        p = masked_softmax(jnp.concatenate([logits, sink], axis=-1),
                           jnp.concatenate([mask, jnp.ones(sink.shape, bool)], axis=-1))[..., :-1]
    else:
        p = masked_softmax(logits, mask)
    return jnp.einsum('qkgs,skd->qkgd', p.astype(v.dtype), v)


def window_attention(q, q_pos0, k, v, w_pos0, rel_bias, window, sinks):
    T = q.shape[0]
    pad = window - (q_pos0 - w_pos0)
    k = jnp.pad(k, ((pad, 0), (0, 0), (0, 0)))
    v = jnp.pad(v, ((pad, 0), (0, 0), (0, 0)))
    base = q_pos0 - window
    qb = block_len(T)

    def blk(b):
        qs = b * qb
        q_b = lax.dynamic_slice_in_dim(q, qs, qb, 0)
        k_b = lax.dynamic_slice_in_dim(k, qs, qb + window, 0)
        v_b = lax.dynamic_slice_in_dim(v, qs, qb + window, 0)
        t = q_pos0 + qs + jnp.arange(qb)
        s = base + qs + jnp.arange(qb + window)
        return window_block(q_b, t, k_b, v_b, s, rel_bias, window, sinks)

    o = lax.map(blk, jnp.arange(T // qb))
    return o.reshape((T,) + o.shape[2:])


def compress(rows, pe, w1, w2):
    L = rows.shape[0]
    n_cmp = (L - CMP_LEN) // CMP_STRIDE + 1
    idx = jnp.arange(n_cmp)[:, None] * CMP_STRIDE + jnp.arange(CMP_LEN)[None, :]
    blk = rows[idx] + pe[None, :, None, :]
    flat = jnp.transpose(blk, (0, 2, 1, 3)).reshape(n_cmp, KV_A, CMP_LEN * HEAD_DIM)
    return jax.nn.gelu(flat @ w1) @ w2


def nsa_sparse(q, q_pos0, rows4, cmp_pe, cmp_w1, cmp_w2, rel_bias):
    T = q.shape[0]
    L = rows4.shape[0]
    kc = compress(rows4[:, 0], cmp_pe[0], cmp_w1[0], cmp_w2[0])
    vc = compress(rows4[:, 1], cmp_pe[1], cmp_w1[1], cmp_w2[1])
    n_cmp = kc.shape[0]
    cmp_start = jnp.arange(n_cmp) * CMP_STRIDE
    cmp_end = cmp_start + CMP_LEN - 1
    n_slc = -(-L // SEL_BLOCK)
    pad = n_slc * SEL_BLOCK - L

    def to_blocks(r):
        r = jnp.pad(r, ((0, pad), (0, 0), (0, 0)))
        return jnp.transpose(r.reshape(n_slc, SEL_BLOCK, KV_A, HEAD_DIM), (2, 0, 1, 3))

    ks_blk = to_blocks(rows4[:, 2])
    vs_blk = to_blocks(rows4[:, 3])
    slc_start = jnp.arange(n_slc) * SEL_BLOCK
    overlap = ((cmp_start[:, None] <= slc_start[None, :] + SEL_BLOCK - 1)
               & (cmp_end[:, None] >= slc_start[None, :])).astype(jnp.float32)
    k_top = min(SEL_TOPK, n_slc)
    rb_group = rel_bias.reshape(NUM_BUCKETS, KV_A, G_A)
    kv_ids = jnp.arange(KV_A)[None, :, None]
    qb = block_len(T)

    def blk(b):
        qs = b * qb
        q_b = lax.dynamic_slice_in_dim(q, qs, qb, 0)
        t = q_pos0 + qs + jnp.arange(qb)
        dist_c = t[:, None] - cmp_end[None, :]
        bias_c = jnp.transpose(rel_bias[t5_bucket(dist_c)].reshape(qb, n_cmp, KV_A, G_A), (0, 2, 3, 1))
        logit_c = jnp.einsum('qkgd,nkd->qkgn', q_b, kc).astype(jnp.float32) * SCALE + bias_c
        p_c = masked_softmax(logit_c, (dist_c >= 0)[:, None, None, :])
        o_c = jnp.einsum('qkgn,nkd->qkgd', p_c.astype(vc.dtype), vc)
        imp = jnp.einsum('qkgn,nj->qkj', p_c, overlap)
        jj = jnp.arange(n_slc)[None, :]
        cur = (t // SEL_BLOCK)[:, None]
        valid = slc_start[None, :] <= t[:, None]
        forced = (jj == 0) | ((jj <= cur) & (jj > cur - N_LOCAL_FORCED))
        score = jnp.where(valid[:, None, :], jnp.where(forced[:, None, :], FORCED_SCORE, imp), -jnp.inf)
        top_score, sel = lax.top_k(score, k_top)
        sel_ok = jnp.isfinite(top_score)
        kg = ks_blk[kv_ids, sel].reshape(qb, KV_A, k_top * SEL_BLOCK, HEAD_DIM)
        vg = vs_blk[kv_ids, sel].reshape(qb, KV_A, k_top * SEL_BLOCK, HEAD_DIM)
        s_pos = (sel[..., None] * SEL_BLOCK + jnp.arange(SEL_BLOCK)).reshape(qb, KV_A, k_top * SEL_BLOCK)
        dist_s = t[:, None, None] - s_pos
        ok_s = (dist_s >= 0) & jnp.repeat(sel_ok, SEL_BLOCK, axis=-1)
        bias_s = jax.vmap(lambda bk, tab: tab[bk], in_axes=(1, 1), out_axes=1)(t5_bucket(dist_s), rb_group)
        logit_s = jnp.einsum('qkgd,qksd->qkgs', q_b, kg).astype(jnp.float32) * SCALE + jnp.swapaxes(bias_s, 2, 3)
        p_s = masked_softmax(logit_s, ok_s[:, :, None, :])
        o_s = jnp.einsum('qkgs,qksd->qkgd', p_s.astype(vg.dtype), vg)
        return o_c, o_s

    o_c, o_s = lax.map(blk, jnp.arange(T // qb))
    return o_c.reshape(T, KV_A, G_A, HEAD_DIM), o_s.reshape(T, KV_A, G_A, HEAD_DIM)


def nsa_mixer(xn, w_in, w_out, cmp_pe, cmp_w1, cmp_w2, rel_bias, past):
    B, T, _ = xn.shape
    nq = N_HEADS * HEAD_DIM
    nkv = KV_A * HEAD_DIM
    proj = xn @ w_in
    q = proj[..., :nq].reshape(B, T, KV_A, G_A, HEAD_DIM)
    rows = proj[..., nq:nq + 6 * nkv].reshape(B, T, 6, KV_A, HEAD_DIM)
    gates = jax.nn.sigmoid(proj[..., nq + 6 * nkv:].astype(jnp.float32)).reshape(B, T, KV_A, G_A, N_BRANCH)
    rows4 = rows[:, :, :4]
    win2 = rows[:, :, 4:]

    def seq_fn(q_s, g_s, full4, w2, q_pos0, w_pos0):
        o_c, o_s = nsa_sparse(q_s, q_pos0, full4, cmp_pe, cmp_w1, cmp_w2, rel_bias)
        o_w = window_attention(q_s, q_pos0, w2[:, 0], w2[:, 1], w_pos0, rel_bias, WIN_A, None)
        o = g_s[..., 0:1] * o_c + g_s[..., 1:2] * o_s + g_s[..., 2:3] * o_w
        return o.reshape(T, nq).astype(xn.dtype)

    if past is None:
        o = lax.map(lambda a: seq_fn(a[0], a[1], a[2], a[3], 0, 0), (q, gates, rows4, win2))
        new_win = win2[:, T - min(WIN_A, T):]
    else:
        pool, layer, page_table, win_buf = past
        wb = win_buf.shape[1]

        def samp(a):
            q_s, g_s, new4, neww, pt, wbuf = a
            past4 = pool[layer, pt].reshape(PAST_LEN, 4, KV_A, HEAD_DIM)
            full4 = jnp.concatenate([past4, new4], axis=0)
            w2 = jnp.concatenate([wbuf, neww], axis=0)
            return seq_fn(q_s, g_s, full4, w2, PAST_LEN, PAST_LEN - wb)

        o = lax.map(samp, (q, gates, rows4, win2, page_table, win_buf))
        new_win = jnp.concatenate([win_buf, win2], axis=1)[:, T:]
    return o @ w_out, rows4, new_win


def swa_mixer(xn, w_q, sinks, w_out, kv_shared, rel_bias, past_buf):
    B, T, _ = xn.shape
    q = (xn @ w_q).reshape(B, T, KV_B, G_B, HEAD_DIM)
    if past_buf is None:
        o = lax.map(lambda a: window_attention(a[0], 0, a[1][:, 0], a[1][:, 1], 0, rel_bias, WIN_B, sinks), (q, kv_shared))
    else:
        wb = past_buf.shape[1]

        def samp(a):
            kv = jnp.concatenate([a[2], a[1]], axis=0)
            return window_attention(a[0], PAST_LEN, kv[:, 0], kv[:, 1], PAST_LEN - wb, rel_bias, WIN_B, sinks)

        o = lax.map(samp, (q, kv_shared, past_buf))
    return o.reshape(B, T, N_HEADS * HEAD_DIM) @ w_out


def peer(x, wq, subkeys, u, v):
    n = x.shape[0]
    pad = (-n) % TOK_BLOCK
    xp = jnp.pad(x, ((0, pad), (0, 0))).reshape(-1, TOK_BLOCK, D_MODEL)

    def blk(xb):
        q = (xb @ wq).reshape(TOK_BLOCK, PEER_HEADS, 2, PEER_HALF)
        s = jnp.einsum('nhcd,hckd->nhck', q, subkeys).astype(jnp.float32)
        sv, si = lax.top_k(s, PEER_TOPK)
        cand = (sv[:, :, 0, :, None] + sv[:, :, 1, None, :]).reshape(TOK_BLOCK, PEER_HEADS, PEER_TOPK * PEER_TOPK)
        top, ci = lax.top_k(cand, PEER_TOPK)
        e = (jnp.take_along_axis(si[:, :, 0], ci // PEER_TOPK, axis=-1) * N_KEYS
             + jnp.take_along_axis(si[:, :, 1], ci % PEER_TOPK, axis=-1))
        g = jax.nn.softmax(top, axis=-1)
        a = jax.nn.gelu(jnp.einsum('nd,nhkd->nhk', xb, u[e]).astype(jnp.float32))
        return jnp.einsum('nhk,nhkd->nd', (g * a).astype(xb.dtype), v[e])

    return lax.map(blk, xp).reshape(-1, D_MODEL)[:n]


def run_trunk(x, c, p, past):
    B, T, _ = x.shape
    silu_c = jax.nn.silu(c)
    h = x
    nsa_rows, nsa_wins = [], []
    kv_shared, swa_state = None, None
    for i in range(DEPTH):
        mod = silu_c @ p['ada_w'][i] + p['ada_b'][i]
        sh1, sc1, g1, sh2, sc2, g2 = jnp.split(mod, 6, axis=-1)
        xn = modulate(h, p['norm_mix'][i], sh1, sc1)
        if i < N_A_LAYERS:
            nsa_past = None if past is None else (past[0], i, past[1], past[2][i])
            out, rows4, win_state = nsa_mixer(xn, p['nsa_w_in'][i], p['nsa_w_out'][i], p['nsa_cmp_pe'][i],
                                              p['nsa_cmp_w1'][i], p['nsa_cmp_w2'][i], p['rel_bias'], nsa_past)
            nsa_rows.append(rows4)
            nsa_wins.append(win_state)
        else:
            if kv_shared is None:
                shk, sck = jnp.split(silu_c @ p['kv_ada_w'] + p['kv_ada_b'], 2, axis=-1)
                kv_shared = (modulate(h, p['kv_norm'], shk, sck) @ p['kv_w']).reshape(B, T, 2, KV_B, HEAD_DIM)
                if past is None:
                    swa_state = kv_shared[:, T - min(WIN_B, T):]
                else:
                    swa_state = jnp.concatenate([past[3], kv_shared], axis=1)[:, T:]
            j = i - N_A_LAYERS
            out = swa_mixer(xn, p['swa_w_q'][j], p['swa_sinks'][j], p['swa_w_out'][j], kv_shared, p['rel_bias'],
                            None if past is None else past[3])
        h = h + g1[:, None, :] * out
        xn = modulate(h, p['norm_ffn'][i], sh2, sc2)
        ffn = peer(xn.reshape(B * T, D_MODEL), p['peer_wq'][i], p['peer_subkeys'][i], p['peer_u'][i], p['peer_v'][i])
        h = h + g2[:, None, :] * ffn.reshape(B, T, D_MODEL)
    return rms_norm(h, p['final_norm']), jnp.stack(nsa_rows), jnp.stack(nsa_wins), swa_state


def setup_inputs(seed: int = 0) -> dict:
    key = jax.random.key(seed)
    ks = iter(jax.random.split(key, 40))

    def nrm(shape, s):
        return jax.random.normal(next(ks), shape, jnp.float32) * s

    n_pages = PAST_LEN // PAGE_SIZE
    n_used = DEC_BATCH * n_pages
    n_pool = n_used + n_used // 4
    wb_a = min(WIN_A, PAST_LEN)
    wb_b = min(WIN_B, PAST_LEN)
    dsc = D_MODEL ** -0.5
    inp = {}
    inp['x_prompt'] = nrm((BATCH, SEQ, D_MODEL), 1.0)
    inp['x_sample'] = nrm((DEC_BATCH, DEC_SEQ, D_MODEL), 1.0)
    inp['cache_nsa_kv'] = nrm((N_A_LAYERS, n_pool, PAGE_SIZE, 4, KV_A, HEAD_DIM), 1.0)
    inp['state_nsa_win'] = nrm((N_A_LAYERS, DEC_BATCH, wb_a, 2, KV_A, HEAD_DIM), 1.0)
    inp['state_swa_kv'] = nrm((DEC_BATCH, wb_b, 2, KV_B, HEAD_DIM), 1.0)
    inp['page_table'] = jax.random.permutation(next(ks), n_pool)[:n_used].reshape(DEC_BATCH, n_pages).astype(jnp.int32)
    inp['c_prompt'] = nrm((BATCH, D_MODEL), 1.0)
    inp['c_sample'] = nrm((DEC_BATCH, D_MODEL), 1.0)
    inp['ada_w'] = nrm((DEPTH, D_MODEL, 6 * D_MODEL), 0.3 * dsc)
    inp['ada_b'] = nrm((DEPTH, 6 * D_MODEL), 0.02)
    inp['norm_mix'] = 1.0 + nrm((DEPTH, D_MODEL), 0.02)
    inp['norm_ffn'] = 1.0 + nrm((DEPTH, D_MODEL), 0.02)
    inp['nsa_w_in'] = nrm((N_A_LAYERS, D_MODEL, NSA_IN_DIM), dsc)
    inp['nsa_w_out'] = nrm((N_A_LAYERS, N_HEADS * HEAD_DIM, D_MODEL), dsc)
    inp['nsa_cmp_pe'] = nrm((N_A_LAYERS, 2, CMP_LEN, HEAD_DIM), 0.1)
    inp['nsa_cmp_w1'] = nrm((N_A_LAYERS, 2, CMP_LEN * HEAD_DIM, HEAD_DIM), (CMP_LEN * HEAD_DIM) ** -0.5)
    inp['nsa_cmp_w2'] = nrm((N_A_LAYERS, 2, HEAD_DIM, HEAD_DIM), HEAD_DIM ** -0.5)
    inp['kv_ada_w'] = nrm((D_MODEL, 2 * D_MODEL), 0.3 * dsc)
    inp['kv_ada_b'] = nrm((2 * D_MODEL,), 0.02)
    inp['kv_norm'] = 1.0 + nrm((D_MODEL,), 0.02)
    inp['kv_w'] = nrm((D_MODEL, 2 * KV_B * HEAD_DIM), dsc)
    inp['swa_w_q'] = nrm((N_B_LAYERS, D_MODEL, N_HEADS * HEAD_DIM), dsc)
    inp['swa_sinks'] = nrm((N_B_LAYERS, N_HEADS), 1.0)
    inp['swa_w_out'] = nrm((N_B_LAYERS, N_HEADS * HEAD_DIM, D_MODEL), dsc)
    inp['rel_bias'] = nrm((NUM_BUCKETS, N_HEADS), 0.5)
    inp['peer_wq'] = nrm((DEPTH, D_MODEL, PEER_HEADS * PEER_KEY_DIM), dsc)
    inp['peer_subkeys'] = nrm((DEPTH, PEER_HEADS, 2, N_KEYS, PEER_HALF), PEER_HALF ** -0.5)
    inp['peer_u'] = nrm((DEPTH, N_EXPERTS, D_MODEL), dsc)
    inp['peer_v'] = nrm((DEPTH, N_EXPERTS, D_MODEL), PEER_HEADS ** -0.5)
    inp['final_norm'] = 1.0 + nrm((D_MODEL,), 0.02)
    return inp


def reference(x_prompt, x_sample, cache_nsa_kv, state_nsa_win, state_swa_kv, page_table, c_prompt, c_sample,
              ada_w, ada_b, norm_mix, norm_ffn, nsa_w_in, nsa_w_out, nsa_cmp_pe, nsa_cmp_w1, nsa_cmp_w2,
              kv_ada_w, kv_ada_b, kv_norm, kv_w, swa_w_q, swa_sinks, swa_w_out, rel_bias,
              peer_wq, peer_subkeys, peer_u, peer_v, final_norm):
    p = dict(ada_w=ada_w, ada_b=ada_b, norm_mix=norm_mix, norm_ffn=norm_ffn, nsa_w_in=nsa_w_in,
             nsa_w_out=nsa_w_out, nsa_cmp_pe=nsa_cmp_pe, nsa_cmp_w1=nsa_cmp_w1, nsa_cmp_w2=nsa_cmp_w2,
             kv_ada_w=kv_ada_w, kv_ada_b=kv_ada_b, kv_norm=kv_norm, kv_w=kv_w, swa_w_q=swa_w_q,
             swa_sinks=swa_sinks, swa_w_out=swa_w_out, rel_bias=rel_bias, peer_wq=peer_wq,
             peer_subkeys=peer_subkeys, peer_u=peer_u, peer_v=peer_v, final_norm=final_norm)
    y_prompt, nsa_kv_prompt, nsa_win_prompt, swa_kv_prompt = run_trunk(x_prompt, c_prompt, p, None)
    y_sample, nsa_kv_sample, nsa_win_sample, swa_kv_sample = run_trunk(
        x_sample, c_sample, p, (cache_nsa_kv, page_table, state_nsa_win, state_swa_kv))
    return (y_prompt, y_sample, nsa_kv_prompt, nsa_kv_sample, nsa_win_prompt, nsa_win_sample, swa_kv_prompt, swa_kv_sample)
```

```python
import functools
import math

import jax
import jax.numpy as jnp
from jax import lax
from jax.experimental import pallas as pl
from jax.experimental.pallas import tpu as pltpu

F32 = jnp.float32
I32 = jnp.int32
CDT = jnp.bfloat16

N_HEADS = 16
HEAD_DIM = 128
KV = 2
G = N_HEADS // KV
SCALE = HEAD_DIM ** -0.5
CMP_LEN = 32
CMP_STRIDE = 16
SEL_BLOCK = 64
SEL_TOPK = 16
N_LOCAL_FORCED = 2
WIN_A = 512
WIN_B = 128
N_BRANCH = 3
FORCED_SCORE = 1e9
NUM_BUCKETS = 32
MAX_EXACT = 16
MAX_DISTANCE = 128
PEER_HEADS = 8
N_KEYS = 128
PEER_TOPK = 16
EPS = 1e-6
SEL_SHIFT = SEL_BLOCK.bit_length() - 1
TOPK_SHIFT = PEER_TOPK.bit_length() - 1
QB = 128
DEC_T = 8
LANE = 128
NEG = -jnp.inf

VMEM_LIMIT = 56 * 1024 * 1024


def _cp(sem=None, vmem=VMEM_LIMIT):
    return pltpu.CompilerParams(dimension_semantics=sem, vmem_limit_bytes=vmem)


def _nt(a, b):
    return lax.dot_general(a, b, (((1,), (1,)), ((), ())), preferred_element_type=F32)


def _dot(a, b):
    return jnp.dot(a, b, preferred_element_type=F32)


def _iota(shape, axis):
    return lax.broadcasted_iota(I32, shape, axis)


def _mm_kernel(x_ref, w_ref, b_ref, o_ref, *, act_in, act_out):
    x = x_ref[...]
    if act_in == "silu":
        x = x.astype(F32)
        x = x * jax.nn.sigmoid(x)
    acc = _dot(x.astype(CDT), w_ref[...].astype(CDT)) + b_ref[...]
    if act_out == "sigmoid":
        acc = jax.nn.sigmoid(acc)
    o_ref[...] = acc.astype(o_ref.dtype)


def matmul(x, w, b=None, *, act_in=None, act_out=None, out_dtype=F32, tm=512, tn=512):
    M, K = x.shape
    N = w.shape[1]
    tm = tm if M % tm == 0 else M
    tn = tn if N % tn == 0 else N
    if b is None:
        b = jnp.zeros((N,), F32)
    b = b.reshape(1, N).astype(F32)
    return pl.pallas_call(
        functools.partial(_mm_kernel, act_in=act_in, act_out=act_out),
        out_shape=jax.ShapeDtypeStruct((M, N), out_dtype),
        grid=(N // tn, M // tm),
        in_specs=[pl.BlockSpec((tm, K), lambda j, i: (i, 0)),
                  pl.BlockSpec((K, tn), lambda j, i: (0, j)),
                  pl.BlockSpec((1, tn), lambda j, i: (0, j))],
        out_specs=pl.BlockSpec((tm, tn), lambda j, i: (i, j)),
        compiler_params=_cp(("parallel", "parallel")),
        name="matmul",
    )(x, w, b)


def _resmod_kernel(*refs, has_res, mod_flags, emit_h):
    it = iter(refs)
    h = next(it)[...]
    if has_res:
        o_ref, g_ref = next(it), next(it)
        h = h + g_ref[...] * o_ref[...]
    mods = []
    for has_mod in mod_flags:
        gain = next(it)
        mods.append((gain, next(it), next(it)) if has_mod else (gain, None, None))
    if emit_h:
        next(it)[...] = h
    y = h * lax.rsqrt(jnp.mean(h * h, axis=-1, keepdims=True) + EPS)
    for gain, sh, sc in mods:
        z = y * gain[...]
        if sh is not None:
            z = z * (1.0 + sc[...]) + sh[...]
        out = next(it)
        out[...] = z.reshape(out.shape).astype(out.dtype)


def resmod(h, res, mods, out_dtypes, *, emit_h, gb=16):
    NG, R, D = h.shape
    gb = gb if NG % gb == 0 else NG
    tok = pl.BlockSpec((gb, R, D), lambda i: (i, 0, 0))
    tok2 = pl.BlockSpec((gb * R, D), lambda i: (i, 0))

    def chunk(c):
        return pl.BlockSpec((gb, 1, D), lambda i, c=c: (i, 0, c))

    args, specs = [h], [tok]
    if res is not None:
        out, (garr, gc) = res
        args += [out, garr]
        specs += [tok, chunk(gc)]
    flags = []
    for gain, m in mods:
        args.append(gain.reshape(1, 1, D))
        specs.append(pl.BlockSpec((1, 1, D), lambda i: (0, 0, 0)))
        flags.append(m is not None)
        if m is not None:
            marr, shc, scc = m
            args += [marr, marr]
            specs += [chunk(shc), chunk(scc)]
    out_shape, out_specs = [], []
    if emit_h:
        out_shape.append(jax.ShapeDtypeStruct(h.shape, F32))
        out_specs.append(tok)
    for dt in out_dtypes:
        out_shape.append(jax.ShapeDtypeStruct((NG * R, D), dt))
        out_specs.append(tok2)
    return pl.pallas_call(
        functools.partial(_resmod_kernel, has_res=res is not None, mod_flags=tuple(flags), emit_h=emit_h),
        out_shape=out_shape, grid=(NG // gb,), in_specs=specs, out_specs=out_specs,
        compiler_params=_cp(("parallel",)), name="resmod",
    )(*args)


def _t5_bucket(dist):
    n = jnp.maximum(dist, 0)
    nf = jnp.maximum(n, 1).astype(F32)
    large = MAX_EXACT + (jnp.log(nf / MAX_EXACT) / math.log(MAX_DISTANCE / MAX_EXACT)
                         * (NUM_BUCKETS - MAX_EXACT)).astype(I32)
    large = jnp.minimum(large, NUM_BUCKETS - 1)
    return jnp.where(n < MAX_EXACT, n, large)


def _bias_lookup(bucket, rb_ref, h):
    out = jnp.full(bucket.shape, rb_ref[NUM_BUCKETS - 1, h], F32)
    for b in range(NUM_BUCKETS - 1):
        out = jnp.where(bucket == b, rb_ref[b, h], out)
    return out


def _bias_tile_kernel(rb_ref, o_ref, *, delta0, dstep, cs, coff, window, ncols_valid):
    ti = pl.program_id(0)
    h = pl.program_id(1)
    shape = o_ref.shape[-2:]
    i = _iota(shape, 0)
    j = _iota(shape, 1)
    dist = delta0 + ti * dstep + i - (j * cs + coff)
    ok = dist >= 0
    if window is not None:
        ok = ok & (dist <= window)
    if ncols_valid is not None:
        ok = ok & (j < ncols_valid)
    o_ref[0, 0] = jnp.where(ok, _bias_lookup(_t5_bucket(dist), rb_ref, h), NEG)


def bias_tiles(rel_bias, n_tiles, rows, cols, *, delta0=0, dstep=0, cs=1, coff=0, window=None, ncols_valid=None):
    return pl.pallas_call(
        functools.partial(_bias_tile_kernel, delta0=delta0, dstep=dstep, cs=cs, coff=coff, window=window,
                          ncols_valid=ncols_valid),
        out_shape=jax.ShapeDtypeStruct((n_tiles, N_HEADS, rows, cols), F32),
        grid=(n_tiles, N_HEADS),
        in_specs=[pl.BlockSpec(memory_space=pltpu.SMEM)],
        out_specs=pl.BlockSpec((1, 1, rows, cols), lambda t, h: (t, h, 0, 0)),
        compiler_params=_cp(("parallel", "parallel")), name="bias_tiles",
    )(rel_bias)


def _softmax_rows(s):
    m = jnp.max(s, axis=-1, keepdims=True)
    m = jnp.where(m == NEG, 0.0, m)
    e = jnp.exp(s - m)
    z = jnp.sum(e, axis=-1, keepdims=True)
    return e / jnp.where(z > 0, z, 1.0)


def _flash_step(s, v, m, l, acc):
    gg, r, kk = s.shape
    m_new = jnp.maximum(m, jnp.max(s, axis=-1, keepdims=True))
    m_safe = jnp.where(m_new == NEG, 0.0, m_new)
    alpha = jnp.exp(m - m_safe)
    p = jnp.exp(s - m_safe)
    l = alpha * l + jnp.sum(p, axis=-1, keepdims=True)
    pv = _dot(p.reshape(gg * r, kk).astype(CDT), v).reshape(gg, r, HEAD_DIM)
    return m_new, l, alpha * acc + pv


def _flash_finish(m, l, acc, sinks):
    if sinks is not None:
        m_f = jnp.maximum(m, sinks)
        a = jnp.exp(m - m_f)
        l = l * a + jnp.exp(sinks - m_f)
        acc = acc * a
    return acc / jnp.where(l > 0, l, 1.0)


def _head_rows(q, kv):
    return jnp.concatenate([q[:, (kv * G + g) * HEAD_DIM:(kv * G + g + 1) * HEAD_DIM] for g in range(G)],
                           axis=0).astype(CDT)


def _sink_col(sink_ref, kv):
    return jnp.concatenate([jnp.full((1, 1, 1), sink_ref[kv * G + g], F32) for g in range(G)], axis=0)


def _flash_prompt_kernel(*refs, n_tiles, max_back, has_sel, has_sinks):
    it = iter(refs)
    q_ref, k_ref, v_ref, t_ref = next(it), next(it), next(it), next(it)
    sel_ref = next(it) if has_sel else None
    sink_ref = next(it) if has_sinks else None
    o_ref = next(it)
    qi = pl.program_id(0)
    lo = 0 if max_back is None else jnp.maximum(qi - max_back, 0)
    q = q_ref[...]
    for kv in range(KV):
        q2 = _head_rows(q, kv)
        cols = slice(kv * HEAD_DIM, (kv + 1) * HEAD_DIM)

        def body(c, carry, kv=kv, q2=q2, cols=cols):
            m, l, acc = carry
            start = pl.multiple_of(c * QB, QB)
            ks = k_ref[pl.ds(start, QB), cols]
            vs = v_ref[pl.ds(start, QB), cols]
            s = _nt(q2, ks).reshape(G, QB, QB) * SCALE
            s = s + t_ref[jnp.minimum(qi - c, n_tiles - 1), kv * G:(kv + 1) * G]
            if has_sel:
                nj = sel_ref.shape[-1]
                e = (_iota((nj, QB), 0) == (QB // SEL_BLOCK) * c + (_iota((nj, QB), 1) >> SEL_SHIFT)).astype(CDT)
                hit = _dot(sel_ref[kv], e) > 0.5
                s = jnp.where(hit[None], s, NEG)
            return _flash_step(s, vs, m, l, acc)

        init = (jnp.full((G, QB, 1), NEG, F32), jnp.zeros((G, QB, 1), F32), jnp.zeros((G, QB, HEAD_DIM), F32))
        m, l, acc = lax.fori_loop(lo, qi + 1, body, init)
        o = _flash_finish(m, l, acc, _sink_col(sink_ref, kv) if has_sinks else None)
        for g in range(G):
            o_ref[:, (kv * G + g) * HEAD_DIM:(kv * G + g + 1) * HEAD_DIM] = o[g]


def flash_prompt(q, kvarr, kblk, vblk, tiles, *, n_q, max_back, sel=None, sinks=None):
    T = kvarr.shape[0]
    n_tiles = tiles.shape[0]
    w = KV * HEAD_DIM
    args = [q, kvarr, kvarr, tiles]
    specs = [pl.BlockSpec((QB, N_HEADS * HEAD_DIM), lambda i: (i, 0)),
             pl.BlockSpec((T, w), lambda i: (0, kblk)),
             pl.BlockSpec((T, w), lambda i: (0, vblk)),
             pl.BlockSpec(tiles.shape, lambda i: (0, 0, 0, 0))]
    if sel is not None:
        args.append(sel)
        specs.append(pl.BlockSpec((KV, QB, sel.shape[-1]), lambda i: (0, i, 0)))
    if sinks is not None:
        args.append(sinks)
        specs.append(pl.BlockSpec(memory_space=pltpu.SMEM))
    return pl.pallas_call(
        functools.partial(_flash_prompt_kernel, n_tiles=n_tiles, max_back=max_back, has_sel=sel is not None,
                          has_sinks=sinks is not None),
        out_shape=jax.ShapeDtypeStruct((n_q, N_HEADS * HEAD_DIM), F32),
        grid=(n_q // QB,), in_specs=specs,
        out_specs=pl.BlockSpec((QB, N_HEADS * HEAD_DIM), lambda i: (i, 0)),
        compiler_params=_cp(("parallel",)), name="flash_prompt",
    )(*args)


def _compress_rows(x, pe_ref, w1_ref, acc_ref, first):
    @pl.when(first)
    def _():
        acc_ref[...] = jnp.zeros(acc_ref.shape, F32)

    for gi in range(2 * KV):
        comp = gi // KV
        xg = x[:, gi * HEAD_DIM:(gi + 1) * HEAD_DIM]
        for a in range(2):
            xa = (xg + pe_ref[comp, a, 0]).astype(CDT)
            col = slice((gi * 2 + a) * HEAD_DIM, (gi * 2 + a + 1) * HEAD_DIM)
            acc_ref[:, col] += _dot(xa, w1_ref[comp, a, 0].astype(CDT))


def _compress_finish(p, w2_ref, gi):
    c = p.shape[0]
    pa = p[:, (gi * 2) * HEAD_DIM:(gi * 2 + 1) * HEAD_DIM]
    pb = p[:, (gi * 2 + 1) * HEAD_DIM:(gi * 2 + 2) * HEAD_DIM]
    h1 = jax.nn.gelu(pa + pltpu.roll(pb, c - 1, 0))
    return _dot(h1.astype(CDT), w2_ref[gi // KV].astype(CDT))


def _compress_prompt_kernel(x_ref, pe_ref, w1_ref, w2_ref, o_ref, acc_ref):
    b = pl.program_id(0)
    _compress_rows(x_ref[...], pe_ref, w1_ref, acc_ref, b == 0)

    @pl.when(b == CMP_STRIDE - 1)
    def _():
        p = acc_ref[...]
        for gi in range(2 * KV):
            o_ref[gi] = _compress_finish(p, w2_ref, gi)


def _cmp_weights(pe, w1):
    pe5 = pe.reshape(2, 2, CMP_STRIDE, 1, HEAD_DIM)
    w15 = w1.reshape(2, 2, CMP_STRIDE, HEAD_DIM, HEAD_DIM)
    return pe5, w15


def compress_prompt(rows, pe, w1, w2):
    T = rows.shape[0]
    C = T // CMP_STRIDE
    x = rows.reshape(C, CMP_STRIDE * rows.shape[1])
    nblk = rows.shape[1] // (4 * HEAD_DIM)
    pe5, w15 = _cmp_weights(pe, w1)
    return pl.pallas_call(
        _compress_prompt_kernel,
        out_shape=jax.ShapeDtypeStruct((2 * KV, C, HEAD_DIM), F32),
        grid=(CMP_STRIDE,),
        in_specs=[pl.BlockSpec((C, 4 * HEAD_DIM), lambda b: (0, b * nblk)),
                  pl.BlockSpec((2, 2, 1, 1, HEAD_DIM), lambda b: (0, 0, b, 0, 0)),
                  pl.BlockSpec((2, 2, 1, HEAD_DIM, HEAD_DIM), lambda b: (0, 0, b, 0, 0)),
                  pl.BlockSpec((2, HEAD_DIM, HEAD_DIM), lambda b: (0, 0, 0))],
        out_specs=pl.BlockSpec((2 * KV, C, HEAD_DIM), lambda b: (0, 0, 0)),
        scratch_shapes=[pltpu.VMEM((C, 8 * HEAD_DIM), F32)],
        compiler_params=_cp(("arbitrary",)), name="compress_prompt",
    )(x, pe5, w15, w2)


def _overlap(n_idx, j_idx, n_cmp, n_slc):
    return ((n_idx * CMP_STRIDE <= j_idx * SEL_BLOCK + SEL_BLOCK - 1)
            & (n_idx * CMP_STRIDE + CMP_LEN - 1 >= j_idx * SEL_BLOCK)
            & (n_idx < n_cmp) & (j_idx < n_slc))


def _split_hi_lo(x):
    hi = x.astype(CDT)
    return hi, (x - hi.astype(F32)).astype(CDT)


def _sel_scores(imp, j_idx, t, n_slc):
    valid = (j_idx * SEL_BLOCK <= t) & (j_idx < n_slc)
    cur = t >> SEL_SHIFT
    forced = (j_idx == 0) | ((j_idx <= cur) & (j_idx > cur - N_LOCAL_FORCED))
    return jnp.where(valid, jnp.where(forced, FORCED_SCORE, imp), NEG)


def _topk_mask(score, idx, axis, k):
    n = score.shape[axis]
    sel = jnp.zeros(score.shape, F32)
    for _ in range(k):
        m = jnp.max(score, axis=axis, keepdims=True)
        first = jnp.min(jnp.where(score == m, idx, n), axis=axis, keepdims=True)
        hit = idx == first
        sel = jnp.where(hit & (m > NEG), 1.0, sel)
        score = jnp.where(hit, NEG, score)
    return sel


def _cmp_prompt_kernel(q_ref, kc_ref, rb_ref, oc_ref, sel_ref, *, n_cmp, n_slc, k_top):
    qi = pl.program_id(0)
    t0 = qi * QB
    C = kc_ref.shape[1]
    J = sel_ref.shape[-1]
    n_idx = _iota((QB, C), 1)
    dist = t0 + _iota((QB, C), 0) - (n_idx * CMP_STRIDE + CMP_LEN - 1)
    ok = (dist >= 0) & (n_idx < n_cmp)
    bucket = _t5_bucket(dist)
    ov_t = _overlap(_iota((J, C), 1), _iota((J, C), 0), n_cmp, n_slc).astype(CDT)
    j_t = _iota((J, QB), 0)
    t_t = t0 + _iota((J, QB), 1)
    q = q_ref[...]
    for kv in range(KV):
        q2 = _head_rows(q, kv)
        s = _nt(q2, kc_ref[kv].astype(CDT)).reshape(G, QB, C) * SCALE
        bias = jnp.stack([_bias_lookup(bucket, rb_ref, kv * G + g) for g in range(G)], axis=0)
        p = _softmax_rows(jnp.where(ok[None], s + bias, NEG))
        o = _dot(p.reshape(G * QB, C).astype(CDT), kc_ref[KV + kv].astype(CDT)).reshape(G, QB, HEAD_DIM)
        for g in range(G):
            oc_ref[:, (kv * G + g) * HEAD_DIM:(kv * G + g + 1) * HEAD_DIM] = o[g]
        hi, lo = _split_hi_lo(jnp.sum(p, axis=0))
        imp_t = _nt(ov_t, hi) + _nt(ov_t, lo)
        sel_t = _topk_mask(_sel_scores(imp_t, j_t, t_t, n_slc), j_t, 0, k_top)
        sel_ref[kv] = sel_t.T.astype(sel_ref.dtype)


def cmp_prompt(q, kcvc, rel_bias, *, n_q):
    C = kcvc.shape[1]
    n_cmp = (n_q - CMP_LEN) // CMP_STRIDE + 1
    n_slc = -(-n_q // SEL_BLOCK)
    return pl.pallas_call(
        functools.partial(_cmp_prompt_kernel, n_cmp=n_cmp, n_slc=n_slc, k_top=min(SEL_TOPK, n_slc)),
        out_shape=[jax.ShapeDtypeStruct((n_q, N_HEADS * HEAD_DIM), F32),
                   jax.ShapeDtypeStruct((KV, n_q, n_slc), CDT)],
        grid=(n_q // QB,),
        in_specs=[pl.BlockSpec((QB, N_HEADS * HEAD_DIM), lambda i: (i, 0)),
                  pl.BlockSpec(kcvc.shape, lambda i: (0, 0, 0)),
                  pl.BlockSpec(memory_space=pltpu.SMEM)],
        out_specs=[pl.BlockSpec((QB, N_HEADS * HEAD_DIM), lambda i: (i, 0)),
                   pl.BlockSpec((KV, QB, n_slc), lambda i: (0, i, 0))],
        compiler_params=_cp(("parallel",)), name="cmp_prompt",
    )(q, kcvc, rel_bias)


def _combine_kernel(g_ref, c_ref, s_ref, w_ref, o_ref):
    gates = g_ref[...]
    for h in range(N_HEADS):
        col = slice(h * HEAD_DIM, (h + 1) * HEAD_DIM)
        o = (gates[:, N_BRANCH * h:N_BRANCH * h + 1] * c_ref[:, col]
             + gates[:, N_BRANCH * h + 1:N_BRANCH * h + 2] * s_ref[:, col]
             + gates[:, N_BRANCH * h + 2:N_BRANCH * h + 3] * w_ref[:, col])
        o_ref[:, col] = o.astype(o_ref.dtype)


def combine(gates, oc, os_, ow, tm=256):
    N, D = oc.shape
    assert N % tm == 0
    tok = pl.BlockSpec((tm, D), lambda i: (i, 0))
    return pl.pallas_call(
        _combine_kernel, out_shape=jax.ShapeDtypeStruct((N, D), CDT), grid=(N // tm,),
        in_specs=[pl.BlockSpec((tm, gates.shape[1]), lambda i: (i, 0)), tok, tok, tok], out_specs=tok,
        compiler_params=_cp(("parallel",)), name="nsa_combine",
    )(gates, oc, os_, ow)


def _pool_cmp_kernel(x_ref, pe_ref, w1_ref, o_ref):
    _compress_rows(x_ref[...], pe_ref, w1_ref, o_ref, pl.program_id(1) == 0)


def pool_compress(pool2d, pe, w1, tc=2048):
    n_chunks = pool2d.shape[0]
    tc = tc if n_chunks % tc == 0 else n_chunks
    pe5, w15 = _cmp_weights(pe, w1)
    return pl.pallas_call(
        _pool_cmp_kernel,
        out_shape=jax.ShapeDtypeStruct((n_chunks, 8 * HEAD_DIM), F32),
        grid=(n_chunks // tc, CMP_STRIDE),
        in_specs=[pl.BlockSpec((tc, 4 * HEAD_DIM), lambda i, b: (i, 2 * b)),
                  pl.BlockSpec((2, 2, 1, 1, HEAD_DIM), lambda i, b: (0, 0, b, 0, 0)),
                  pl.BlockSpec((2, 2, 1, HEAD_DIM, HEAD_DIM), lambda i, b: (0, 0, b, 0, 0))],
        out_specs=pl.BlockSpec((tc, 8 * HEAD_DIM), lambda i, b: (i, 0)),
        compiler_params=_cp(("parallel", "arbitrary")), name="pool_compress",
    )(pool2d, pe5, w15)


def _cmp_sample_kernel(pt_ref, *refs, n_ops, n_cmp, n_slc, k_top, q_pos0):
    p_refs = refs[:n_ops]
    q_ref, w2_ref, t_ref, oc_ref, sel_ref, pbuf = refs[n_ops:]
    gstep = pl.program_id(1)
    cpp = p_refs[0].shape[1]
    for r in range(n_ops):
        pbuf[pl.ds(pl.multiple_of((gstep * n_ops + r) * cpp, cpp), cpp), :] = p_refs[r][0]

    @pl.when(gstep == pl.num_programs(1) - 1)
    def _():
        p = pbuf[...]
        C = p.shape[0]
        J = sel_ref.shape[-1]
        q = q_ref[0]
        ov = _overlap(_iota((C, J), 0), _iota((C, J), 1), n_cmp, n_slc).astype(CDT)
        j_idx = _iota((DEC_T, J), 1)
        t = q_pos0 + _iota((DEC_T, J), 0)
        for kv in range(KV):
            kc = _compress_finish(p, w2_ref, kv).astype(CDT)
            vc = _compress_finish(p, w2_ref, KV + kv).astype(CDT)
            q2 = _head_rows(q, kv)
            s = _nt(q2, kc).reshape(G, DEC_T, C) * SCALE + t_ref[0, kv * G:(kv + 1) * G]
            pr = _softmax_rows(s)
            o = _dot(pr.reshape(G * DEC_T, C).astype(CDT), vc)
            for g in range(G):
                oc_ref[0, :, (kv * G + g) * HEAD_DIM:(kv * G + g + 1) * HEAD_DIM] = o[g * DEC_T:(g + 1) * DEC_T]
            hi, lo = _split_hi_lo(jnp.sum(pr, axis=0))
            imp = _dot(hi, ov) + _dot(lo, ov)
            sel_ref[0, kv * DEC_T:(kv + 1) * DEC_T, :] = _topk_mask(_sel_scores(imp, j_idx, t, n_slc), j_idx, 1, k_top)


def cmp_sample(page_table, pool_p, q3, w2, tile_c, *, n_ops=8, jp=256):
    B, n_pages = page_table.shape
    cpp = pool_p.shape[1]
    C = n_pages * cpp
    past = C * CMP_STRIDE
    L = past + DEC_T
    n_cmp = (L - CMP_LEN) // CMP_STRIDE + 1
    n_slc = -(-L // SEL_BLOCK)
    n_ops = n_ops if n_pages % n_ops == 0 else n_pages
    D = N_HEADS * HEAD_DIM
    p_specs = [pl.BlockSpec((1, cpp, 8 * HEAD_DIM), lambda b, g, pt, r=r: (pt[b, g * n_ops + r], 0, 0))
               for r in range(n_ops)]
    grid_spec = pltpu.PrefetchScalarGridSpec(
        num_scalar_prefetch=1, grid=(B, n_pages // n_ops),
        in_specs=p_specs + [pl.BlockSpec((1, DEC_T, D), lambda b, g, pt: (b, 0, 0)),
                            pl.BlockSpec(w2.shape, lambda b, g, pt: (0, 0, 0)),
                            pl.BlockSpec(tile_c.shape, lambda b, g, pt: (0, 0, 0, 0))],
        out_specs=[pl.BlockSpec((1, DEC_T, D), lambda b, g, pt: (b, 0, 0)),
                   pl.BlockSpec((1, KV * DEC_T, jp), lambda b, g, pt: (b, 0, 0))],
        scratch_shapes=[pltpu.VMEM((C, 8 * HEAD_DIM), F32)])
    return pl.pallas_call(
        functools.partial(_cmp_sample_kernel, n_ops=n_ops, n_cmp=n_cmp, n_slc=n_slc, k_top=min(SEL_TOPK, n_slc),
                          q_pos0=past),
        out_shape=[jax.ShapeDtypeStruct((B, DEC_T, D), F32), jax.ShapeDtypeStruct((B, KV * DEC_T, jp), F32)],
        grid_spec=grid_spec, compiler_params=_cp(("parallel", "arbitrary")), name="cmp_sample",
    )(page_table, *([pool_p] * n_ops), q3, w2, tile_c)


def _sel_sample_kernel(pt_ref, *refs, n_ops, n_pages):
    p_refs = refs[:n_ops]
    q_ref, sel_ref, new_ref, tp_ref, tn_ref, o_ref, m_ref, l_ref, acc_ref = refs[n_ops:]
    gstep = pl.program_id(1)
    page = p_refs[0].shape[1]
    bpp = page // SEL_BLOCK
    J = sel_ref.shape[-1]

    @pl.when(gstep == 0)
    def _():
        m_ref[...] = jnp.full(m_ref.shape, NEG, F32)
        l_ref[...] = jnp.zeros(l_ref.shape, F32)
        acc_ref[...] = jnp.zeros(acc_ref.shape, F32)

    q = q_ref[0]

    def update(kv, k, v, bias, blk0):
        nk = k.shape[0]
        e = (_iota((J, nk), 0) == blk0 + (_iota((J, nk), 1) >> SEL_SHIFT)).astype(CDT)
        hit = _dot(sel_ref[0, kv * DEC_T:(kv + 1) * DEC_T, :].astype(CDT), e) > 0.5
        s = _nt(_head_rows(q, kv), k).reshape(G, DEC_T, nk) * SCALE + bias
        s = jnp.where(hit[None], s, NEG)
        m, l, acc = _flash_step(s, v, m_ref[kv], l_ref[kv], acc_ref[kv])
        m_ref[kv] = m
        l_ref[kv] = l
        acc_ref[kv] = acc

    pages = [p_refs[r][0] for r in range(n_ops)]
    for kv in range(KV):
        k = jnp.concatenate([pg[:, kv * HEAD_DIM:(kv + 1) * HEAD_DIM] for pg in pages], axis=0).astype(CDT)
        v = jnp.concatenate([pg[:, (KV + kv) * HEAD_DIM:(KV + kv + 1) * HEAD_DIM] for pg in pages], axis=0).astype(CDT)
        bias = jnp.concatenate(
            [tp_ref[jnp.where(gstep * n_ops + r == n_pages - 1, 1, 0), kv * G:(kv + 1) * G] for r in range(n_ops)],
            axis=-1)
        update(kv, k, v, bias, gstep * n_ops * bpp)

    @pl.when(gstep == pl.num_programs(1) - 1)
    def _():
        new = new_ref[0]
        pad = jnp.zeros((page - DEC_T, HEAD_DIM), F32)
        for kv in range(KV):
            k = jnp.concatenate([new[:, kv * HEAD_DIM:(kv + 1) * HEAD_DIM], pad], axis=0).astype(CDT)
            v = jnp.concatenate([new[:, (KV + kv) * HEAD_DIM:(KV + kv + 1) * HEAD_DIM], pad], axis=0).astype(CDT)
            update(kv, k, v, tn_ref[0, kv * G:(kv + 1) * G], n_pages * bpp)
            o = _flash_finish(m_ref[kv], l_ref[kv], acc_ref[kv], None)
            for g in range(G):
                o_ref[0, :, (kv * G + g) * HEAD_DIM:(kv * G + g + 1) * HEAD_DIM] = o[g]


def sel_sample(page_table, pool3, q3, sel, rows3, tiles_past, tile_new, *, g0, n_ops=8):
    B, n_pages = page_table.shape
    page = pool3.shape[1]
    n_ops = n_ops if n_pages % n_ops == 0 else n_pages
    D = N_HEADS * HEAD_DIM
    w = 2 * KV * HEAD_DIM
    p_specs = [pl.BlockSpec((1, page, w), lambda b, g, pt, r=r: (pt[b, g * n_ops + r], 0, 1)) for r in range(n_ops)]
    grid_spec = pltpu.PrefetchScalarGridSpec(
        num_scalar_prefetch=1, grid=(B, n_pages // n_ops),
        in_specs=p_specs + [pl.BlockSpec((1, DEC_T, D), lambda b, g, pt: (b, 0, 0)),
                            pl.BlockSpec((1,) + sel.shape[1:], lambda b, g, pt: (b, 0, 0)),
                            pl.BlockSpec((1, DEC_T, w), lambda b, g, pt: (g0 + b, 0, 1)),
                            pl.BlockSpec(tiles_past.shape, lambda b, g, pt: (0, 0, 0, 0)),
                            pl.BlockSpec(tile_new.shape, lambda b, g, pt: (0, 0, 0, 0))],
        out_specs=pl.BlockSpec((1, DEC_T, D), lambda b, g, pt: (b, 0, 0)),
        scratch_shapes=[pltpu.VMEM((KV, G, DEC_T, 1), F32), pltpu.VMEM((KV, G, DEC_T, 1), F32),
                        pltpu.VMEM((KV, G, DEC_T, HEAD_DIM), F32)])
    return pl.pallas_call(
        functools.partial(_sel_sample_kernel, n_ops=n_ops, n_pages=n_pages),
        out_shape=jax.ShapeDtypeStruct((B, DEC_T, D), F32),
        grid_spec=grid_spec, compiler_params=_cp(("parallel", "arbitrary")), name="sel_sample",
    )(page_table, *([pool3] * n_ops), q3, sel, rows3, tiles_past, tile_new)


def _win_sample_kernel(*refs, has_sinks):
    it = iter(refs)
    q_ref, buf_ref, new_ref, tb_ref, tn_ref = next(it), next(it), next(it), next(it), next(it)
    sink_ref = next(it) if has_sinks else None
    o_ref, st_ref = next(it), next(it)
    q = q_ref[0]
    buf = buf_ref[0]
    new = new_ref[0]
    wb = buf.shape[0]
    npad = tn_ref.shape[-1]
    pad = jnp.zeros((npad - DEC_T, HEAD_DIM), F32)
    for kv in range(KV):
        kc = slice(kv * HEAD_DIM, (kv + 1) * HEAD_DIM)
        vc = slice((KV + kv) * HEAD_DIM, (KV + kv + 1) * HEAD_DIM)
        k = jnp.concatenate([buf[:, kc], new[:, kc], pad], axis=0).astype(CDT)
        v = jnp.concatenate([buf[:, vc], new[:, vc], pad], axis=0).astype(CDT)
        bias = jnp.concatenate([tb_ref[0, kv * G:(kv + 1) * G], tn_ref[0, kv * G:(kv + 1) * G]], axis=-1)
        s = _nt(_head_rows(q, kv), k).reshape(G, DEC_T, wb + npad) * SCALE + bias
        init = (jnp.full((G, DEC_T, 1), NEG, F32), jnp.zeros((G, DEC_T, 1), F32), jnp.zeros((G, DEC_T, HEAD_DIM), F32))
        m, l, acc = _flash_step(s, v, *init)
        o = _flash_finish(m, l, acc, _sink_col(sink_ref, kv) if has_sinks else None)
        for g in range(G):
            o_ref[0, :, (kv * G + g) * HEAD_DIM:(kv * G + g + 1) * HEAD_DIM] = o[g]
    st_ref[0, :wb - DEC_T, :] = buf[DEC_T:, :]
    st_ref[0, wb - DEC_T:, :] = new


def win_sample(q3, buf, new3, new_blk, tile_buf, tile_new, *, g0, sinks=None):
    B, wb, w = buf.shape
    D = N_HEADS * HEAD_DIM
    args = [q3, buf, new3, tile_buf, tile_new]
    specs = [pl.BlockSpec((1, DEC_T, D), lambda b: (b, 0, 0)),
             pl.BlockSpec((1, wb, w), lambda b: (b, 0, 0)),
             pl.BlockSpec((1, DEC_T, w), lambda b: (g0 + b, 0, new_blk)),
             pl.BlockSpec(tile_buf.shape, lambda b: (0, 0, 0, 0)),
             pl.BlockSpec(tile_new.shape, lambda b: (0, 0, 0, 0))]
    if sinks is not None:
        args.append(sinks)
        specs.append(pl.BlockSpec(memory_space=pltpu.SMEM))
    return pl.pallas_call(
        functools.partial(_win_sample_kernel, has_sinks=sinks is not None),
        out_shape=[jax.ShapeDtypeStruct((B, DEC_T, D), F32), jax.ShapeDtypeStruct(buf.shape, F32)],
        grid=(B,), in_specs=specs,
        out_specs=[pl.BlockSpec((1, DEC_T, D), lambda b: (b, 0, 0)), pl.BlockSpec((1, wb, w), lambda b: (b, 0, 0))],
        compiler_params=_cp(("parallel",)), name="win_sample",
    )(*args)


def _topk_rows(s, k):
    r = s.shape[0]
    idx = _iota(s.shape, 0)
    vals, ids = [], []
    for _ in range(k):
        m = jnp.max(s, axis=0, keepdims=True)
        first = jnp.min(jnp.where(s == m, idx, r), axis=0, keepdims=True)
        vals.append(m)
        ids.append(first)
        s = jnp.where(idx == first, NEG, s)
    return jnp.concatenate(vals, axis=0), jnp.concatenate(ids, axis=0)


def _peer_topk_kernel(q_ref, sk_ref, ik_ref, jk_ref, g_ref):
    half = sk_ref.shape[-1]
    sv, si = [], []
    for c in range(2):
        s_t = _nt(sk_ref[0, c].astype(CDT), q_ref[:, c * half:(c + 1) * half].astype(CDT))
        v, i = _topk_rows(s_t, PEER_TOPK)
        sv.append(v)
        si.append(i)
    cand = jnp.concatenate([sv[0][a:a + 1] + sv[1] for a in range(PEER_TOPK)], axis=0)
    top, ci = _topk_rows(cand, PEER_TOPK)
    a_idx = ci >> TOPK_SHIFT
    b_idx = ci & (PEER_TOPK - 1)
    isel = jnp.zeros(ci.shape, I32)
    jsel = jnp.zeros(ci.shape, I32)
    for a in range(PEER_TOPK):
        isel = jnp.where(a_idx == a, si[0][a:a + 1], isel)
        jsel = jnp.where(b_idx == a, si[1][a:a + 1], jsel)
    e = jnp.exp(top - top[0:1])
    ik_ref[...] = isel
    jk_ref[...] = jsel
    g_ref[...] = e / jnp.sum(e, axis=0, keepdims=True)


def peer_topk(q, subkeys, tm=256):
    N = q.shape[0]
    assert N % tm == 0
    half = subkeys.shape[-1]
    outs = pl.BlockSpec((PEER_TOPK, tm), lambda i, h: (h, i))
    shp = (PEER_HEADS * PEER_TOPK, N)
    return pl.pallas_call(
        _peer_topk_kernel,
        out_shape=[jax.ShapeDtypeStruct(shp, I32), jax.ShapeDtypeStruct(shp, I32), jax.ShapeDtypeStruct(shp, F32)],
        grid=(N // tm, PEER_HEADS),
        in_specs=[pl.BlockSpec((tm, 2 * half), lambda i, h: (i, h)),
                  pl.BlockSpec((1, 2, N_KEYS, half), lambda i, h: (h, 0, 0, 0))],
        out_specs=[outs, outs, outs],
        compiler_params=_cp(("parallel", "parallel")), name="peer_topk",
    )(q, subkeys)


def _peer_w_kernel(ik_ref, jk_ref, g_ref, w_ref, iks, jks, gs):
    iks[...] = ik_ref[...].T
    jks[...] = jk_ref[...].T
    gs[...] = g_ref[...].T
    hk = iks.shape[1]
    key = _iota((N_KEYS, hk), 0)

    def body(n, carry):
        a = jnp.where(key == iks[pl.ds(n, 1), :], gs[pl.ds(n, 1), :], 0.0).astype(CDT)
        b = jnp.where(key == jks[pl.ds(n, 1), :], 1.0, 0.0).astype(CDT)
        w_ref[n] = _nt(a, b).astype(w_ref.dtype)
        return carry

    lax.fori_loop(0, iks.shape[0], body, 0)


def peer_weights(ik, jk, g, tm=128):
    hk, N = ik.shape
    assert N % tm == 0
    blk = pl.BlockSpec((hk, tm), lambda i: (0, i))
    return pl.pallas_call(
        _peer_w_kernel,
        out_shape=jax.ShapeDtypeStruct((N, N_KEYS, N_KEYS), CDT),
        grid=(N // tm,), in_specs=[blk, blk, blk],
        out_specs=pl.BlockSpec((tm, N_KEYS, N_KEYS), lambda i: (i, 0, 0)),
        scratch_shapes=[pltpu.VMEM((tm, hk), I32), pltpu.VMEM((tm, hk), I32), pltpu.VMEM((tm, hk), F32)],
        compiler_params=_cp(("parallel",)), name="peer_weights",
    )(ik, jk, g)


def _peer_dense_kernel(x_ref, u_ref, v_ref, w_ref, o_ref):
    @pl.when(pl.program_id(1) == 0)
    def _():
        o_ref[...] = jnp.zeros(o_ref.shape, F32)

    h = _nt(x_ref[...], u_ref[...].astype(CDT))
    a = jax.nn.gelu(h) * w_ref[...].astype(F32)
    o_ref[...] += _dot(a.astype(CDT), v_ref[...].astype(CDT))


def peer_dense(x, u, v, w, tm=1024, te=256):
    N, D = x.shape
    E = u.shape[0]
    tm = tm if N % tm == 0 else N
    return pl.pallas_call(
        _peer_dense_kernel,
        out_shape=jax.ShapeDtypeStruct((N, D), F32),
        grid=(N // tm, E // te),
        in_specs=[pl.BlockSpec((tm, D), lambda i, e: (i, 0)),
                  pl.BlockSpec((te, D), lambda i, e: (e, 0)),
                  pl.BlockSpec((te, D), lambda i, e: (e, 0)),
                  pl.BlockSpec((tm, te), lambda i, e: (i, e))],
        out_specs=pl.BlockSpec((tm, D), lambda i, e: (i, 0)),
        compiler_params=_cp(("parallel", "arbitrary")), name="peer_dense",
    )(x, u, v, w)


def peer(xn2d, wq, subkeys, u, v):
    q = matmul(xn2d, wq)
    ik, jk, g = peer_topk(q, subkeys)
    w = peer_weights(ik, jk, g)
    return peer_dense(xn2d, u, v, w.reshape(w.shape[0], N_KEYS * N_KEYS))


def _group_rows(mod, n_prompt_groups):
    w = mod.shape[1]
    return jnp.concatenate([jnp.broadcast_to(mod[0:1], (n_prompt_groups, w)), mod[1:]], axis=0)[:, None, :]


def nsa_mixer(xn, T, w_in, w_out, cmp_pe, cmp_w1, cmp_w2, rel_bias, pool, page_table, win_state):
    N = xn.shape[0]
    B, n_pages = page_table.shape
    n_pg = T // DEC_T
    nq, nkv = N_HEADS * HEAD_DIM, KV * HEAD_DIM
    n_pool, page = pool.shape[0], pool.shape[1]
    past = n_pages * page
    wb = win_state.shape[1]
    assert wb == WIN_A and page % CMP_STRIDE == 0 and page % SEL_BLOCK == 0 and page >= MAX_DISTANCE

    q = matmul(xn, w_in[:, :nq], out_dtype=CDT)
    rows = matmul(xn, w_in[:, nq:nq + 6 * nkv])
    w_gate = jnp.pad(w_in[:, nq + 6 * nkv:], ((0, 0), (0, LANE - N_BRANCH * N_HEADS)))
    gates = matmul(xn, w_gate, act_out="sigmoid")
    rows3 = rows.reshape(N // DEC_T, DEC_T, 6 * nkv)
    q_s = q[T:].astype(F32).reshape(B, DEC_T, nq)

    rows_bf_p = rows[:T].astype(CDT)
    kcvc = compress_prompt(rows[:T], cmp_pe, cmp_w1, cmp_w2)
    oc_p, sel_p = cmp_prompt(q, kcvc, rel_bias, n_q=T)
    tiles_sel = bias_tiles(rel_bias, 3, QB, QB, dstep=QB)
    os_p = flash_prompt(q, rows_bf_p, 2, 3, tiles_sel, n_q=T, max_back=None, sel=sel_p)
    tiles_w = bias_tiles(rel_bias, WIN_A // QB + 1, QB, QB, dstep=QB, window=WIN_A)
    ow_p = flash_prompt(q, rows_bf_p, 4, 5, tiles_w, n_q=T, max_back=WIN_A // QB)

    pool3 = pool.reshape(n_pool, page, 4 * nkv)
    cpp = page // CMP_STRIDE
    pool_p = pool_compress(pool3.reshape(n_pool * cpp, CMP_STRIDE * 4 * nkv), cmp_pe, cmp_w1)
    n_cmp_s = (past + DEC_T - CMP_LEN) // CMP_STRIDE + 1
    tile_c = bias_tiles(rel_bias, 1, DEC_T, past // CMP_STRIDE, delta0=past, cs=CMP_STRIDE, coff=CMP_LEN - 1,
                        ncols_valid=n_cmp_s)
    oc_s, sel_s = cmp_sample(page_table, pool_p.reshape(n_pool, cpp, 8 * HEAD_DIM), q_s, cmp_w2, tile_c)
    tiles_past = bias_tiles(rel_bias, 2, DEC_T, page, delta0=2 * page, dstep=-page)
    tile_new = bias_tiles(rel_bias, 1, DEC_T, page)
    os_s = sel_sample(page_table, pool3, q_s, sel_s, rows3, tiles_past, tile_new, g0=n_pg)
    tile_buf = bias_tiles(rel_bias, 1, DEC_T, wb, delta0=wb, window=WIN_A)
    tile_wnew = bias_tiles(rel_bias, 1, DEC_T, LANE, window=WIN_A)
    ow_s, new_win_s = win_sample(q_s, win_state.reshape(B, wb, 2 * nkv), rows3, 2, tile_buf, tile_wnew, g0=n_pg)

    def both(p, s):
        return jnp.concatenate([p, s.reshape(B * DEC_T, nq)], axis=0)

    o = combine(gates, both(oc_p, oc_s), both(os_p, os_s), both(ow_p, ow_s))
    return matmul(o, w_out), rows, new_win_s


def swa_mixer(xn, T, kvsh, w_q, sinks, w_out, rel_bias, state):
    N = xn.shape[0]
    B, wb = state.shape[0], state.shape[1]
    assert wb == WIN_B
    nq, nkv = N_HEADS * HEAD_DIM, KV * HEAD_DIM
    q = matmul(xn, w_q, out_dtype=CDT)
    tiles = bias_tiles(rel_bias, WIN_B // QB + 1, QB, QB, dstep=QB, window=WIN_B)
    o_p = flash_prompt(q, kvsh[:T].astype(CDT), 0, 1, tiles, n_q=T, max_back=WIN_B // QB, sinks=sinks)
    tile_buf = bias_tiles(rel_bias, 1, DEC_T, wb, delta0=wb, window=WIN_B)
    tile_new = bias_tiles(rel_bias, 1, DEC_T, LANE, window=WIN_B)
    o_s, new_state = win_sample(q[T:].astype(F32).reshape(B, DEC_T, nq), state.reshape(B, wb, 2 * nkv),
                                kvsh.reshape(N // DEC_T, DEC_T, 2 * nkv), 0, tile_buf, tile_new, g0=T // DEC_T,
                                sinks=sinks)
    o = jnp.concatenate([o_p, o_s.reshape(B * DEC_T, nq)], axis=0).astype(CDT)
    return matmul(o, w_out), new_state


def kernel(x_prompt, x_sample, cache_nsa_kv, state_nsa_win, state_swa_kv, page_table, c_prompt, c_sample,
           ada_w, ada_b, norm_mix, norm_ffn, nsa_w_in, nsa_w_out, nsa_cmp_pe, nsa_cmp_w1, nsa_cmp_w2,
           kv_ada_w, kv_ada_b, kv_norm, kv_w, swa_w_q, swa_sinks, swa_w_out, rel_bias,
           peer_wq, peer_subkeys, peer_u, peer_v, final_norm):
    assert x_prompt.shape[0] == 1 and x_sample.shape[1] == DEC_T
    T, D = x_prompt.shape[1], x_prompt.shape[2]
    B = x_sample.shape[0]
    assert T % QB == 0 and D == N_HEADS * HEAD_DIM
    n_pg = T // DEC_T
    NG = n_pg + B
    N = NG * DEC_T
    nq = N_HEADS * HEAD_DIM
    nkv = KV * HEAD_DIM
    wb_a, wb_b = state_nsa_win.shape[2], state_swa_kv.shape[1]
    assert state_nsa_win.shape[0] == 1 and cache_nsa_kv.shape[0] == 1 and ada_w.shape[0] == 2

    h = jnp.concatenate([x_prompt.reshape(T, D), x_sample.reshape(B * DEC_T, D)], axis=0).reshape(NG, DEC_T, D)
    c = jnp.concatenate([c_prompt, c_sample], axis=0)
    mods = [_group_rows(matmul(c, ada_w[i], ada_b[i], act_in="silu"), n_pg) for i in range(2)]
    mod_kv = _group_rows(matmul(c, kv_ada_w, kv_ada_b, act_in="silu"), n_pg)

    (xn,) = resmod(h, None, [(norm_mix[0], (mods[0], 0, 1))], [CDT], emit_h=False)
    out, rows, new_win_s = nsa_mixer(xn, T, nsa_w_in[0], nsa_w_out[0], nsa_cmp_pe[0], nsa_cmp_w1[0], nsa_cmp_w2[0],
                                     rel_bias, cache_nsa_kv[0], page_table, state_nsa_win[0])
    h, xn = resmod(h, (out.reshape(NG, DEC_T, D), (mods[0], 2)), [(norm_ffn[0], (mods[0], 3, 4))], [CDT], emit_h=True)
    ffn = peer(xn, peer_wq[0], peer_subkeys[0], peer_u[0], peer_v[0])

    h, xn, xkv = resmod(h, (ffn.reshape(NG, DEC_T, D), (mods[0], 5)),
                        [(norm_mix[1], (mods[1], 0, 1)), (kv_norm, (mod_kv, 0, 1))], [CDT, CDT], emit_h=True)
    kvsh = matmul(xkv, kv_w)
    out, new_swa_s = swa_mixer(xn, T, kvsh, swa_w_q[0], swa_sinks[0], swa_w_out[0], rel_bias, state_swa_kv)
    h, xn = resmod(h, (out.reshape(NG, DEC_T, D), (mods[1], 2)), [(norm_ffn[1], (mods[1], 3, 4))], [CDT], emit_h=True)
    ffn = peer(xn, peer_wq[1], peer_subkeys[1], peer_u[1], peer_v[1])
    (y2d,) = resmod(h, (ffn.reshape(NG, DEC_T, D), (mods[1], 5)), [(final_norm, None)], [F32], emit_h=False)

    rows_p, rows_s = rows[:T], rows[T:]
    return (y2d[:T].reshape(1, T, D),
            y2d[T:].reshape(B, DEC_T, D),
            rows_p[:, :4 * nkv].reshape(1, 1, T, 4, KV, HEAD_DIM),
            rows_s[:, :4 * nkv].reshape(1, B, DEC_T, 4, KV, HEAD_DIM),
            rows_p[T - min(WIN_A, T):, 4 * nkv:].reshape(1, 1, min(WIN_A, T), 2, KV, HEAD_DIM),
            new_win_s.reshape(1, B, wb_a, 2, KV, HEAD_DIM),
            kvsh[T - min(WIN_B, T):T].reshape(1, min(WIN_B, T), 2, KV, HEAD_DIM),
            new_swa_s.reshape(B, wb_b, 2, KV, HEAD_DIM))
```

```python
import functools
import math

import jax
import jax.numpy as jnp
from jax import lax
from jax.experimental import pallas as pl
from jax.experimental.pallas import tpu as pltpu

F32 = jnp.float32
I32 = jnp.int32
CDT = jnp.bfloat16

N_HEADS = 16
HEAD_DIM = 128
KV = 2
G = N_HEADS // KV
SCALE = HEAD_DIM ** -0.5
CMP_LEN = 32
CMP_STRIDE = 16
SEL_BLOCK = 64
SEL_TOPK = 16
N_LOCAL_FORCED = 2
WIN_A = 512
WIN_B = 128
N_BRANCH = 3
FORCED_SCORE = 1e9
NUM_BUCKETS = 32
MAX_EXACT = 16
MAX_DISTANCE = 128
PEER_HEADS = 8
N_KEYS = 128
PEER_TOPK = 16
EPS = 1e-6
SEL_SHIFT = SEL_BLOCK.bit_length() - 1
TOPK_SHIFT = PEER_TOPK.bit_length() - 1
QB = 128
DEC_T = 8
LANE = 128
NEG = -jnp.inf

VMEM_LIMIT = 56 * 1024 * 1024


def _cp(sem=None, vmem=VMEM_LIMIT):
    return pltpu.CompilerParams(dimension_semantics=sem, vmem_limit_bytes=vmem)


def _nt(a, b):
    return lax.dot_general(a, b, (((1,), (1,)), ((), ())), preferred_element_type=F32)


def _dot(a, b):
    return jnp.dot(a, b, preferred_element_type=F32)


def _iota(shape, axis):
    return lax.broadcasted_iota(I32, shape, axis)


def _mm_kernel(x_ref, w_ref, b_ref, o_ref, *, act_in, act_out):
    x = x_ref[...]
    if act_in == "silu":
        x = x.astype(F32)
        x = x * jax.nn.sigmoid(x)
    acc = _dot(x.astype(CDT), w_ref[...].astype(CDT)) + b_ref[...]
    if act_out == "sigmoid":
        acc = jax.nn.sigmoid(acc)
    o_ref[...] = acc.astype(o_ref.dtype)


def matmul(x, w, b=None, *, act_in=None, act_out=None, out_dtype=F32, tm=512, tn=512):
    M, K = x.shape
    N = w.shape[1]
    tm = tm if M % tm == 0 else M
    tn = tn if N % tn == 0 else N
    if b is None:
        b = jnp.zeros((N,), F32)
    b = b.reshape(1, N).astype(F32)
    return pl.pallas_call(
        functools.partial(_mm_kernel, act_in=act_in, act_out=act_out),
        out_shape=jax.ShapeDtypeStruct((M, N), out_dtype),
        grid=(N // tn, M // tm),
        in_specs=[pl.BlockSpec((tm, K), lambda j, i: (i, 0)),
                  pl.BlockSpec((K, tn), lambda j, i: (0, j)),
                  pl.BlockSpec((1, tn), lambda j, i: (0, j))],
        out_specs=pl.BlockSpec((tm, tn), lambda j, i: (i, j)),
        compiler_params=_cp(("parallel", "parallel")),
        name="matmul",
    )(x, w, b)


def _resmod_kernel(*refs, has_res, mod_flags, emit_h):
    it = iter(refs)
    h = next(it)[...]
    if has_res:
        o_ref, g_ref = next(it), next(it)
        h = h + g_ref[...] * o_ref[...]
    mods = []
    for has_mod in mod_flags:
        gain = next(it)
        mods.append((gain, next(it), next(it)) if has_mod else (gain, None, None))
    if emit_h:
        next(it)[...] = h
    y = h * lax.rsqrt(jnp.mean(h * h, axis=-1, keepdims=True) + EPS)
    for gain, sh, sc in mods:
        z = y * gain[...]
        if sh is not None:
            z = z * (1.0 + sc[...]) + sh[...]
        out = next(it)
        out[...] = z.reshape(out.shape).astype(out.dtype)


def resmod(h, res, mods, out_dtypes, *, emit_h, gb=16):
    NG, R, D = h.shape
    gb = gb if NG % gb == 0 else NG
    tok = pl.BlockSpec((gb, R, D), lambda i: (i, 0, 0))
    tok2 = pl.BlockSpec((gb * R, D), lambda i: (i, 0))

    def chunk(c):
        return pl.BlockSpec((gb, 1, D), lambda i, c=c: (i, 0, c))

    args, specs = [h], [tok]
    if res is not None:
        out, (garr, gc) = res
        args += [out, garr]
        specs += [tok, chunk(gc)]
    flags = []
    for gain, m in mods:
        args.append(gain.reshape(1, 1, D))
        specs.append(pl.BlockSpec((1, 1, D), lambda i: (0, 0, 0)))
        flags.append(m is not None)
        if m is not None:
            marr, shc, scc = m
            args += [marr, marr]
            specs += [chunk(shc), chunk(scc)]
    out_shape, out_specs = [], []
    if emit_h:
        out_shape.append(jax.ShapeDtypeStruct(h.shape, F32))
        out_specs.append(tok)
    for dt in out_dtypes:
        out_shape.append(jax.ShapeDtypeStruct((NG * R, D), dt))
        out_specs.append(tok2)
    return pl.pallas_call(
        functools.partial(_resmod_kernel, has_res=res is not None, mod_flags=tuple(flags), emit_h=emit_h),
        out_shape=out_shape, grid=(NG // gb,), in_specs=specs, out_specs=out_specs,
        compiler_params=_cp(("parallel",)), name="resmod",
    )(*args)


def _t5_bucket(dist):
    n = jnp.maximum(dist, 0)
    nf = jnp.maximum(n, 1).astype(F32)
    large = MAX_EXACT + (jnp.log(nf / MAX_EXACT) / math.log(MAX_DISTANCE / MAX_EXACT)
                         * (NUM_BUCKETS - MAX_EXACT)).astype(I32)
    large = jnp.minimum(large, NUM_BUCKETS - 1)
    return jnp.where(n < MAX_EXACT, n, large)


def _bias_lookup(bucket, rb_ref, h):
    out = jnp.full(bucket.shape, rb_ref[NUM_BUCKETS - 1, h], F32)
    for b in range(NUM_BUCKETS - 1):
        out = jnp.where(bucket == b, rb_ref[b, h], out)
    return out


def _bias_tile_kernel(rb_ref, o_ref, *, delta0, dstep, cs, coff, window, ncols_valid):
    ti = pl.program_id(0)
    h = pl.program_id(1)
    shape = o_ref.shape[-2:]
    i = _iota(shape, 0)
    j = _iota(shape, 1)
    dist = delta0 + ti * dstep + i - (j * cs + coff)
    ok = dist >= 0
    if window is not None:
        ok = ok & (dist <= window)
    if ncols_valid is not None:
        ok = ok & (j < ncols_valid)
    o_ref[0, 0] = jnp.where(ok, _bias_lookup(_t5_bucket(dist), rb_ref, h), NEG)


def bias_tiles(rel_bias, n_tiles, rows, cols, *, delta0=0, dstep=0, cs=1, coff=0, window=None, ncols_valid=None):
    return pl.pallas_call(
        functools.partial(_bias_tile_kernel, delta0=delta0, dstep=dstep, cs=cs, coff=coff, window=window,
                          ncols_valid=ncols_valid),
        out_shape=jax.ShapeDtypeStruct((n_tiles, N_HEADS, rows, cols), F32),
        grid=(n_tiles, N_HEADS),
        in_specs=[pl.BlockSpec(memory_space=pltpu.SMEM)],
        out_specs=pl.BlockSpec((1, 1, rows, cols), lambda t, h: (t, h, 0, 0)),
        compiler_params=_cp(("parallel", "parallel")), name="bias_tiles",
    )(rel_bias)


def _softmax_rows(s):
    m = jnp.max(s, axis=-1, keepdims=True)
    m = jnp.where(m == NEG, 0.0, m)
    e = jnp.exp(s - m)
    z = jnp.sum(e, axis=-1, keepdims=True)
    return e / jnp.where(z > 0, z, 1.0)


def _flash_step(s, v, m, l, acc):
    gg, r, kk = s.shape
    m_new = jnp.maximum(m, jnp.max(s, axis=-1, keepdims=True))
    m_safe = jnp.where(m_new == NEG, 0.0, m_new)
    alpha = jnp.exp(m - m_safe)
    p = jnp.exp(s - m_safe)
    l = alpha * l + jnp.sum(p, axis=-1, keepdims=True)
    pv = _dot(p.reshape(gg * r, kk).astype(CDT), v).reshape(gg, r, HEAD_DIM)
    return m_new, l, alpha * acc + pv


def _flash_finish(m, l, acc, sinks):
    if sinks is not None:
        m_f = jnp.maximum(m, sinks)
        a = jnp.exp(m - m_f)
        l = l * a + jnp.exp(sinks - m_f)
        acc = acc * a
    return acc / jnp.where(l > 0, l, 1.0)


def _head_rows(q, kv):
    return jnp.concatenate([q[:, (kv * G + g) * HEAD_DIM:(kv * G + g + 1) * HEAD_DIM] for g in range(G)],
                           axis=0).astype(CDT)


def _sink_col(sink_ref, kv):
    return jnp.concatenate([jnp.full((1, 1, 1), sink_ref[kv * G + g], F32) for g in range(G)], axis=0)


def _flash_prompt_kernel(*refs, n_tiles, max_back, has_sel, has_sinks):
    it = iter(refs)
    q_ref, k_ref, v_ref, t_ref = next(it), next(it), next(it), next(it)
    sel_ref = next(it) if has_sel else None
    sink_ref = next(it) if has_sinks else None
    o_ref = next(it)
    qi = pl.program_id(0)
    lo = 0 if max_back is None else jnp.maximum(qi - max_back, 0)
    q = q_ref[...]
    for kv in range(KV):
        q2 = _head_rows(q, kv)
        cols = slice(kv * HEAD_DIM, (kv + 1) * HEAD_DIM)

        def body(c, carry, kv=kv, q2=q2, cols=cols):
            m, l, acc = carry
            start = pl.multiple_of(c * QB, QB)
            ks = k_ref[pl.ds(start, QB), cols]
            vs = v_ref[pl.ds(start, QB), cols]
            s = _nt(q2, ks).reshape(G, QB, QB) * SCALE
            s = s + t_ref[jnp.minimum(qi - c, n_tiles - 1), kv * G:(kv + 1) * G]
            if has_sel:
                nj = sel_ref.shape[-1]
                e = (_iota((nj, QB), 0) == (QB // SEL_BLOCK) * c + (_iota((nj, QB), 1) >> SEL_SHIFT)).astype(CDT)
                hit = _dot(sel_ref[kv], e) > 0.5
                s = jnp.where(hit[None], s, NEG)
            return _flash_step(s, vs, m, l, acc)

        init = (jnp.full((G, QB, 1), NEG, F32), jnp.zeros((G, QB, 1), F32), jnp.zeros((G, QB, HEAD_DIM), F32))
        m, l, acc = lax.fori_loop(lo, qi + 1, body, init)
        o = _flash_finish(m, l, acc, _sink_col(sink_ref, kv) if has_sinks else None)
        for g in range(G):
            o_ref[:, (kv * G + g) * HEAD_DIM:(kv * G + g + 1) * HEAD_DIM] = o[g]


def flash_prompt(q, kvarr, kblk, vblk, tiles, *, n_q, max_back, sel=None, sinks=None):
    T = kvarr.shape[0]
    n_tiles = tiles.shape[0]
    w = KV * HEAD_DIM
    args = [q, kvarr, kvarr, tiles]
    specs = [pl.BlockSpec((QB, N_HEADS * HEAD_DIM), lambda i: (i, 0)),
             pl.BlockSpec((T, w), lambda i: (0, kblk)),
             pl.BlockSpec((T, w), lambda i: (0, vblk)),
             pl.BlockSpec(tiles.shape, lambda i: (0, 0, 0, 0))]
    if sel is not None:
        args.append(sel)
        specs.append(pl.BlockSpec((KV, QB, sel.shape[-1]), lambda i: (0, i, 0)))
    if sinks is not None:
        args.append(sinks)
        specs.append(pl.BlockSpec(memory_space=pltpu.SMEM))
    return pl.pallas_call(
        functools.partial(_flash_prompt_kernel, n_tiles=n_tiles, max_back=max_back, has_sel=sel is not None,
                          has_sinks=sinks is not None),
        out_shape=jax.ShapeDtypeStruct((n_q, N_HEADS * HEAD_DIM), F32),
        grid=(n_q // QB,), in_specs=specs,
        out_specs=pl.BlockSpec((QB, N_HEADS * HEAD_DIM), lambda i: (i, 0)),
        compiler_params=_cp(("parallel",)), name="flash_prompt",
    )(*args)


def _compress_rows(x, pe_ref, w1_ref, acc_ref, first):
    @pl.when(first)
    def _():
        acc_ref[...] = jnp.zeros(acc_ref.shape, F32)

    for gi in range(2 * KV):
        comp = gi // KV
        xg = x[:, gi * HEAD_DIM:(gi + 1) * HEAD_DIM]
        for a in range(2):
            xa = (xg + pe_ref[comp, a, 0]).astype(CDT)
            col = slice((gi * 2 + a) * HEAD_DIM, (gi * 2 + a + 1) * HEAD_DIM)
            acc_ref[:, col] += _dot(xa, w1_ref[comp, a, 0].astype(CDT))


def _compress_finish(p, w2_ref, gi):
    c = p.shape[0]
    pa = p[:, (gi * 2) * HEAD_DIM:(gi * 2 + 1) * HEAD_DIM]
    pb = p[:, (gi * 2 + 1) * HEAD_DIM:(gi * 2 + 2) * HEAD_DIM]
    h1 = jax.nn.gelu(pa + pltpu.roll(pb, c - 1, 0))
    return _dot(h1.astype(CDT), w2_ref[gi // KV].astype(CDT))


def _compress_prompt_kernel(x_ref, pe_ref, w1_ref, w2_ref, o_ref, acc_ref):
    b = pl.program_id(0)
    _compress_rows(x_ref[...], pe_ref, w1_ref, acc_ref, b == 0)

    @pl.when(b == CMP_STRIDE - 1)
    def _():
        p = acc_ref[...]
        for gi in range(2 * KV):
            o_ref[gi] = _compress_finish(p, w2_ref, gi)


def _cmp_weights(pe, w1):
    pe5 = pe.reshape(2, 2, CMP_STRIDE, 1, HEAD_DIM)
    w15 = w1.reshape(2, 2, CMP_STRIDE, HEAD_DIM, HEAD_DIM)
    return pe5, w15


def compress_prompt(rows, pe, w1, w2):
    T = rows.shape[0]
    C = T // CMP_STRIDE
    x = rows.reshape(C, CMP_STRIDE * rows.shape[1])
    nblk = rows.shape[1] // (4 * HEAD_DIM)
    pe5, w15 = _cmp_weights(pe, w1)
    return pl.pallas_call(
        _compress_prompt_kernel,
        out_shape=jax.ShapeDtypeStruct((2 * KV, C, HEAD_DIM), F32),
        grid=(CMP_STRIDE,),
        in_specs=[pl.BlockSpec((C, 4 * HEAD_DIM), lambda b: (0, b * nblk)),
                  pl.BlockSpec((2, 2, 1, 1, HEAD_DIM), lambda b: (0, 0, b, 0, 0)),
                  pl.BlockSpec((2, 2, 1, HEAD_DIM, HEAD_DIM), lambda b: (0, 0, b, 0, 0)),
                  pl.BlockSpec((2, HEAD_DIM, HEAD_DIM), lambda b: (0, 0, 0))],
        out_specs=pl.BlockSpec((2 * KV, C, HEAD_DIM), lambda b: (0, 0, 0)),
        scratch_shapes=[pltpu.VMEM((C, 8 * HEAD_DIM), F32)],
        compiler_params=_cp(("arbitrary",)), name="compress_prompt",
    )(x, pe5, w15, w2)


def _overlap(n_idx, j_idx, n_cmp, n_slc):
    return ((n_idx * CMP_STRIDE <= j_idx * SEL_BLOCK + SEL_BLOCK - 1)
            & (n_idx * CMP_STRIDE + CMP_LEN - 1 >= j_idx * SEL_BLOCK)
            & (n_idx < n_cmp) & (j_idx < n_slc))


def _split_hi_lo(x):
    hi = x.astype(CDT)
    return hi, (x - hi.astype(F32)).astype(CDT)


def _sel_scores(imp, j_idx, t, n_slc):
    valid = (j_idx * SEL_BLOCK <= t) & (j_idx < n_slc)
    cur = t >> SEL_SHIFT
    forced = (j_idx == 0) | ((j_idx <= cur) & (j_idx > cur - N_LOCAL_FORCED))
    return jnp.where(valid, jnp.where(forced, FORCED_SCORE, imp), NEG)


def _topk_mask(score, idx, axis, k):
    n = score.shape[axis]
    sel = jnp.zeros(score.shape, F32)
    for _ in range(k):
        m = jnp.max(score, axis=axis, keepdims=True)
        first = jnp.min(jnp.where(score == m, idx, n), axis=axis, keepdims=True)
        hit = idx == first
        sel = jnp.where(hit & (m > NEG), 1.0, sel)
        score = jnp.where(hit, NEG, score)
    return sel


def _cmp_prompt_kernel(q_ref, kc_ref, rb_ref, oc_ref, sel_ref, *, n_cmp, n_slc, k_top):
    qi = pl.program_id(0)
    t0 = qi * QB
    C = kc_ref.shape[1]
    J = sel_ref.shape[-1]
    n_idx = _iota((QB, C), 1)
    dist = t0 + _iota((QB, C), 0) - (n_idx * CMP_STRIDE + CMP_LEN - 1)
    ok = (dist >= 0) & (n_idx < n_cmp)
    bucket = _t5_bucket(dist)
    ov_t = _overlap(_iota((J, C), 1), _iota((J, C), 0), n_cmp, n_slc).astype(CDT)
    j_t = _iota((J, QB), 0)
    t_t = t0 + _iota((J, QB), 1)
    q = q_ref[...]
    for kv in range(KV):
        q2 = _head_rows(q, kv)
        s = _nt(q2, kc_ref[kv].astype(CDT)).reshape(G, QB, C) * SCALE
        bias = jnp.stack([_bias_lookup(bucket, rb_ref, kv * G + g) for g in range(G)], axis=0)
        p = _softmax_rows(jnp.where(ok[None], s + bias, NEG))
        o = _dot(p.reshape(G * QB, C).astype(CDT), kc_ref[KV + kv].astype(CDT)).reshape(G, QB, HEAD_DIM)
        for g in range(G):
            oc_ref[:, (kv * G + g) * HEAD_DIM:(kv * G + g + 1) * HEAD_DIM] = o[g]
        hi, lo = _split_hi_lo(jnp.sum(p, axis=0))
        imp_t = _nt(ov_t, hi) + _nt(ov_t, lo)
        sel_t = _topk_mask(_sel_scores(imp_t, j_t, t_t, n_slc), j_t, 0, k_top)
        sel_ref[kv] = sel_t.T.astype(sel_ref.dtype)


def cmp_prompt(q, kcvc, rel_bias, *, n_q):
    C = kcvc.shape[1]
    n_cmp = (n_q - CMP_LEN) // CMP_STRIDE + 1
    n_slc = -(-n_q // SEL_BLOCK)
    return pl.pallas_call(
        functools.partial(_cmp_prompt_kernel, n_cmp=n_cmp, n_slc=n_slc, k_top=min(SEL_TOPK, n_slc)),
        out_shape=[jax.ShapeDtypeStruct((n_q, N_HEADS * HEAD_DIM), F32),
                   jax.ShapeDtypeStruct((KV, n_q, n_slc), CDT)],
        grid=(n_q // QB,),
        in_specs=[pl.BlockSpec((QB, N_HEADS * HEAD_DIM), lambda i: (i, 0)),
                  pl.BlockSpec(kcvc.shape, lambda i: (0, 0, 0)),
                  pl.BlockSpec(memory_space=pltpu.SMEM)],
        out_specs=[pl.BlockSpec((QB, N_HEADS * HEAD_DIM), lambda i: (i, 0)),
                   pl.BlockSpec((KV, QB, n_slc), lambda i: (0, i, 0))],
        compiler_params=_cp(("parallel",)), name="cmp_prompt",
    )(q, kcvc, rel_bias)


def _combine_kernel(g_ref, c_ref, s_ref, w_ref, o_ref):
    gates = g_ref[...]
    for h in range(N_HEADS):
        col = slice(h * HEAD_DIM, (h + 1) * HEAD_DIM)
        o = (gates[:, N_BRANCH * h:N_BRANCH * h + 1] * c_ref[:, col]
             + gates[:, N_BRANCH * h + 1:N_BRANCH * h + 2] * s_ref[:, col]
             + gates[:, N_BRANCH * h + 2:N_BRANCH * h + 3] * w_ref[:, col])
        o_ref[:, col] = o.astype(o_ref.dtype)


def combine(gates, oc, os_, ow, tm=256):
    N, D = oc.shape
    assert N % tm == 0
    tok = pl.BlockSpec((tm, D), lambda i: (i, 0))
    return pl.pallas_call(
        _combine_kernel, out_shape=jax.ShapeDtypeStruct((N, D), CDT), grid=(N // tm,),
        in_specs=[pl.BlockSpec((tm, gates.shape[1]), lambda i: (i, 0)), tok, tok, tok], out_specs=tok,
        compiler_params=_cp(("parallel",)), name="nsa_combine",
    )(gates, oc, os_, ow)


N_SLAB = 4 * KV


def _page_slab(p_ref, slab, row0, n, step):
    return p_ref[0, pl.ds(row0 * N_SLAB + slab, n, stride=step * N_SLAB), :]


def _pool_cmp_kernel(pt_ref, *refs, n_ops, cpp):
    p_refs = refs[:n_ops]
    pe_ref, w1_ref, o_ref = refs[n_ops:]
    for gi in range(2 * KV):
        comp = gi // KV
        acc = [None, None]
        for b in range(CMP_STRIDE):
            x = jnp.concatenate([_page_slab(p, gi, b, cpp, CMP_STRIDE) for p in p_refs], axis=0)
            for a in range(2):
                d = _dot((x + pe_ref[comp, a, b]).astype(CDT), w1_ref[comp, a, b])
                acc[a] = d if acc[a] is None else acc[a] + d
        for a in range(2):
            o_ref[:, (gi * 2 + a) * HEAD_DIM:(gi * 2 + a + 1) * HEAD_DIM] = acc[a]


def pool_compress(page_table, pool3, pe, w1, n_ops=16):
    B, n_pages = page_table.shape
    page = pool3.shape[1] // N_SLAB
    cpp = page // CMP_STRIDE
    n_ops = n_ops if n_pages % n_ops == 0 else n_pages
    ng = n_pages // n_ops
    pe5, w15 = _cmp_weights(pe, w1)
    p_specs = [pl.BlockSpec((1, page * N_SLAB, HEAD_DIM), lambda b, g, pt, r=r: (pt[b, g * n_ops + r], 0, 0))
               for r in range(n_ops)]
    grid_spec = pltpu.PrefetchScalarGridSpec(
        num_scalar_prefetch=1, grid=(B, ng),
        in_specs=p_specs + [pl.BlockSpec(pe5.shape, lambda b, g, pt: (0, 0, 0, 0, 0)),
                            pl.BlockSpec(w15.shape, lambda b, g, pt: (0, 0, 0, 0, 0))],
        out_specs=pl.BlockSpec((n_ops * cpp, 8 * HEAD_DIM), lambda b, g, pt: (b * ng + g, 0)))
    return pl.pallas_call(
        functools.partial(_pool_cmp_kernel, n_ops=n_ops, cpp=cpp),
        out_shape=jax.ShapeDtypeStruct((B * n_pages * cpp, 8 * HEAD_DIM), F32),
        grid_spec=grid_spec, compiler_params=_cp(("parallel", "arbitrary")), name="pool_compress",
    )(page_table, *([pool3] * n_ops), pe5, w15.astype(CDT))


def _topk_rank_mask(score, k, n_valid):
    nr, nj = score.shape
    jr = -(-n_valid // 8) * 8
    st = jnp.concatenate([score, jnp.full((LANE - nr, nj), NEG, F32)], axis=0).T
    idx_c = _iota((jr, nj), 0)
    idx_r = _iota((jr, nj), 1)
    rows = []
    for r in range(nr):
        col = st[:jr, r:r + 1]
        row = score[r:r + 1, :]
        beats = (col > row) | ((col == row) & (idx_c < idx_r))
        rank = jnp.sum(jnp.where(beats, 1.0, 0.0), axis=0, keepdims=True)
        rows.append(jnp.where((rank < k) & (row > NEG), 1.0, 0.0))
    return jnp.concatenate(rows, axis=0)


def _cmp_sample_kernel(p_ref, q_ref, w2_ref, t_ref, e_ref, oc_ref, pen_ref, *, n_cmp, n_slc, k_top, q_pos0):
    p = p_ref[...]
    C = p.shape[0]
    J = e_ref.shape[0]
    q = q_ref[0]
    ov = _overlap(_iota((C, J), 0), _iota((C, J), 1), n_cmp, n_slc).astype(CDT)
    j_idx = _iota((DEC_T, J), 1)
    t = q_pos0 + _iota((DEC_T, J), 0)
    scores = []
    for kv in range(KV):
        kc = _compress_finish(p, w2_ref, kv).astype(CDT)
        vc = _compress_finish(p, w2_ref, KV + kv).astype(CDT)
        q2 = _head_rows(q, kv)
        s = _nt(q2, kc).reshape(G, DEC_T, C) * SCALE + t_ref[0, kv * G:(kv + 1) * G]
        pr = _softmax_rows(s)
        o = _dot(pr.reshape(G * DEC_T, C).astype(CDT), vc)
        for g in range(G):
            oc_ref[0, :, (kv * G + g) * HEAD_DIM:(kv * G + g + 1) * HEAD_DIM] = o[g * DEC_T:(g + 1) * DEC_T]
        hi, lo = _split_hi_lo(jnp.sum(pr, axis=0))
        imp = _dot(hi, ov) + _dot(lo, ov)
        scores.append(_sel_scores(imp, j_idx, t, n_slc))
    sel = _topk_rank_mask(jnp.concatenate(scores, axis=0), k_top, n_slc)
    pen_ref[0] = jnp.where(_dot(sel.astype(CDT), e_ref[...]) > 0.5, 0.0, NEG)


def cmp_sample(pool_p, q3, w2, tile_c, *, page):
    B = q3.shape[0]
    C = pool_p.shape[0] // B
    past = C * CMP_STRIDE
    L = past + DEC_T
    n_cmp = (L - CMP_LEN) // CMP_STRIDE + 1
    n_slc = -(-L // SEL_BLOCK)
    jp = -(-n_slc // LANE) * LANE
    D = N_HEADS * HEAD_DIM
    nk = past + page
    e_mat = (jnp.arange(jp, dtype=I32)[:, None] == jnp.arange(nk, dtype=I32)[None, :] // SEL_BLOCK).astype(CDT)
    return pl.pallas_call(
        functools.partial(_cmp_sample_kernel, n_cmp=n_cmp, n_slc=n_slc, k_top=min(SEL_TOPK, n_slc), q_pos0=past),
        out_shape=[jax.ShapeDtypeStruct((B, DEC_T, D), F32), jax.ShapeDtypeStruct((B, KV * DEC_T, nk), F32)],
        grid=(B,),
        in_specs=[pl.BlockSpec((C, 8 * HEAD_DIM), lambda b: (b, 0)),
                  pl.BlockSpec((1, DEC_T, D), lambda b: (b, 0, 0)),
                  pl.BlockSpec(w2.shape, lambda b: (0, 0, 0)),
                  pl.BlockSpec(tile_c.shape, lambda b: (0, 0, 0, 0)),
                  pl.BlockSpec((jp, nk), lambda b: (0, 0))],
        out_specs=[pl.BlockSpec((1, DEC_T, D), lambda b: (b, 0, 0)),
                   pl.BlockSpec((1, KV * DEC_T, nk), lambda b: (b, 0, 0))],
        compiler_params=_cp(("parallel",)), name="cmp_sample",
    )(pool_p, q3, w2, tile_c, e_mat)


def _sel_sample_kernel(pt_ref, *refs, n_ops, n_pages):
    p_refs = refs[:n_ops]
    q_ref, pen_ref, new_ref, tp_ref, tn_ref, o_ref, m_ref, l_ref, acc_ref = refs[n_ops:]
    gstep = pl.program_id(1)
    page = p_refs[0].shape[1] // N_SLAB
    nk = n_ops * page

    @pl.when(gstep == 0)
    def _():
        m_ref[...] = jnp.full(m_ref.shape, NEG, F32)
        l_ref[...] = jnp.zeros(l_ref.shape, F32)
        acc_ref[...] = jnp.zeros(acc_ref.shape, F32)

    q = q_ref[0]

    def update(kv, k, v, bias, pen):
        s = _nt(_head_rows(q, kv), k).reshape(G, DEC_T, k.shape[0]) * SCALE + bias + pen[None]
        m, l, acc = _flash_step(s, v, m_ref[kv], l_ref[kv], acc_ref[kv])
        m_ref[kv] = m
        l_ref[kv] = l
        acc_ref[kv] = acc

    for kv in range(KV):
        k = jnp.concatenate([_page_slab(p, 2 * KV + kv, 0, page, 1) for p in p_refs], axis=0).astype(CDT)
        v = jnp.concatenate([_page_slab(p, 3 * KV + kv, 0, page, 1) for p in p_refs], axis=0).astype(CDT)
        bias = jnp.concatenate(
            [tp_ref[jnp.where(gstep * n_ops + r == n_pages - 1, 1, 0), kv * G:(kv + 1) * G] for r in range(n_ops)],
            axis=-1)
        pen = pen_ref[0, kv * DEC_T:(kv + 1) * DEC_T, pl.ds(pl.multiple_of(gstep * nk, nk), nk)]
        update(kv, k, v, bias, pen)

    @pl.when(gstep == pl.num_programs(1) - 1)
    def _():
        new = new_ref[0]
        pad = jnp.zeros((page - DEC_T, HEAD_DIM), F32)
        past = n_pages * page
        for kv in range(KV):
            k = jnp.concatenate([new[:, kv * HEAD_DIM:(kv + 1) * HEAD_DIM], pad], axis=0).astype(CDT)
            v = jnp.concatenate([new[:, (KV + kv) * HEAD_DIM:(KV + kv + 1) * HEAD_DIM], pad], axis=0).astype(CDT)
            update(kv, k, v, tn_ref[0, kv * G:(kv + 1) * G], pen_ref[0, kv * DEC_T:(kv + 1) * DEC_T, past:past + page])
            o = _flash_finish(m_ref[kv], l_ref[kv], acc_ref[kv], None)
            for g in range(G):
                o_ref[0, :, (kv * G + g) * HEAD_DIM:(kv * G + g + 1) * HEAD_DIM] = o[g]


def sel_sample(page_table, pool3, q3, pen, rows3, tiles_past, tile_new, *, g0, n_ops=16):
    B, n_pages = page_table.shape
    page = pool3.shape[1] // N_SLAB
    n_ops = n_ops if n_pages % n_ops == 0 else n_pages
    D = N_HEADS * HEAD_DIM
    w = 2 * KV * HEAD_DIM
    p_specs = [pl.BlockSpec((1, page * N_SLAB, HEAD_DIM), lambda b, g, pt, r=r: (pt[b, g * n_ops + r], 0, 0))
               for r in range(n_ops)]
    grid_spec = pltpu.PrefetchScalarGridSpec(
        num_scalar_prefetch=1, grid=(B, n_pages // n_ops),
        in_specs=p_specs + [pl.BlockSpec((1, DEC_T, D), lambda b, g, pt: (b, 0, 0)),
                            pl.BlockSpec((1,) + pen.shape[1:], lambda b, g, pt: (b, 0, 0)),
                            pl.BlockSpec((1, DEC_T, w), lambda b, g, pt: (g0 + b, 0, 1)),
                            pl.BlockSpec(tiles_past.shape, lambda b, g, pt: (0, 0, 0, 0)),
                            pl.BlockSpec(tile_new.shape, lambda b, g, pt: (0, 0, 0, 0))],
        out_specs=pl.BlockSpec((1, DEC_T, D), lambda b, g, pt: (b, 0, 0)),
        scratch_shapes=[pltpu.VMEM((KV, G, DEC_T, 1), F32), pltpu.VMEM((KV, G, DEC_T, 1), F32),
                        pltpu.VMEM((KV, G, DEC_T, HEAD_DIM), F32)])
    return pl.pallas_call(
        functools.partial(_sel_sample_kernel, n_ops=n_ops, n_pages=n_pages),
        out_shape=jax.ShapeDtypeStruct((B, DEC_T, D), F32),
        grid_spec=grid_spec, compiler_params=_cp(("parallel", "arbitrary")), name="sel_sample",
    )(page_table, *([pool3] * n_ops), q3, pen, rows3, tiles_past, tile_new)


def _win_sample_kernel(*refs, has_sinks):
    it = iter(refs)
    q_ref, buf_ref, new_ref, tb_ref, tn_ref = next(it), next(it), next(it), next(it), next(it)
    sink_ref = next(it) if has_sinks else None
    o_ref, st_ref = next(it), next(it)
    q = q_ref[0]
    new = new_ref[0]
    ns = 2 * KV
    wb = buf_ref.shape[1] // ns
    npad = tn_ref.shape[-1]
    pad = jnp.zeros((npad - DEC_T, HEAD_DIM), F32)
    for kv in range(KV):
        kc = slice(kv * HEAD_DIM, (kv + 1) * HEAD_DIM)
        vc = slice((KV + kv) * HEAD_DIM, (KV + kv + 1) * HEAD_DIM)
        k = jnp.concatenate([buf_ref[0, pl.ds(kv, wb, stride=ns), :], new[:, kc], pad], axis=0).astype(CDT)
        v = jnp.concatenate([buf_ref[0, pl.ds(KV + kv, wb, stride=ns), :], new[:, vc], pad], axis=0).astype(CDT)
        bias = jnp.concatenate([tb_ref[0, kv * G:(kv + 1) * G], tn_ref[0, kv * G:(kv + 1) * G]], axis=-1)
        s = _nt(_head_rows(q, kv), k).reshape(G, DEC_T, wb + npad) * SCALE + bias
        init = (jnp.full((G, DEC_T, 1), NEG, F32), jnp.zeros((G, DEC_T, 1), F32), jnp.zeros((G, DEC_T, HEAD_DIM), F32))
        m, l, acc = _flash_step(s, v, *init)
        o = _flash_finish(m, l, acc, _sink_col(sink_ref, kv) if has_sinks else None)
        for g in range(G):
            o_ref[0, :, (kv * G + g) * HEAD_DIM:(kv * G + g + 1) * HEAD_DIM] = o[g]
    st_ref[0, :(wb - DEC_T) * ns, :] = buf_ref[0, DEC_T * ns:, :]
    for s in range(ns):
        st_ref[0, pl.ds((wb - DEC_T) * ns + s, DEC_T, stride=ns), :] = new[:, s * HEAD_DIM:(s + 1) * HEAD_DIM]


def win_sample(q3, buf, new3, new_blk, tile_buf, tile_new, *, g0, sinks=None):
    B, nrow = buf.shape[0], buf.shape[1]
    w = 2 * KV * HEAD_DIM
    D = N_HEADS * HEAD_DIM
    args = [q3, buf, new3, tile_buf, tile_new]
    specs = [pl.BlockSpec((1, DEC_T, D), lambda b: (b, 0, 0)),
             pl.BlockSpec((1, nrow, HEAD_DIM), lambda b: (b, 0, 0)),
             pl.BlockSpec((1, DEC_T, w), lambda b: (g0 + b, 0, new_blk)),
             pl.BlockSpec(tile_buf.shape, lambda b: (0, 0, 0, 0)),
             pl.BlockSpec(tile_new.shape, lambda b: (0, 0, 0, 0))]
    if sinks is not None:
        args.append(sinks)
        specs.append(pl.BlockSpec(memory_space=pltpu.SMEM))
    return pl.pallas_call(
        functools.partial(_win_sample_kernel, has_sinks=sinks is not None),
        out_shape=[jax.ShapeDtypeStruct((B, DEC_T, D), F32), jax.ShapeDtypeStruct(buf.shape, F32)],
        grid=(B,), in_specs=specs,
        out_specs=[pl.BlockSpec((1, DEC_T, D), lambda b: (b, 0, 0)),
                   pl.BlockSpec((1, nrow, HEAD_DIM), lambda b: (b, 0, 0))],
        compiler_params=_cp(("parallel",)), name="win_sample",
    )(*args)


def _topk_rows(s, k):
    r = s.shape[0]
    idx = _iota(s.shape, 0)
    vals, ids = [], []
    for _ in range(k):
        m = jnp.max(s, axis=0, keepdims=True)
        first = jnp.min(jnp.where(s == m, idx, r), axis=0, keepdims=True)
        vals.append(m)
        ids.append(first)
        s = jnp.where(idx == first, NEG, s)
    return jnp.concatenate(vals, axis=0), jnp.concatenate(ids, axis=0)


def _peer_topk_kernel(q_ref, sk_ref, ik_ref, jk_ref, g_ref):
    half = sk_ref.shape[-1]
    sv, si = [], []
    for c in range(2):
        s_t = _nt(sk_ref[0, c].astype(CDT), q_ref[:, c * half:(c + 1) * half].astype(CDT))
        v, i = _topk_rows(s_t, PEER_TOPK)
        sv.append(v)
        si.append(i)
    cand = jnp.concatenate([sv[0][a:a + 1] + sv[1] for a in range(PEER_TOPK)], axis=0)
    top, ci = _topk_rows(cand, PEER_TOPK)
    a_idx = ci >> TOPK_SHIFT
    b_idx = ci & (PEER_TOPK - 1)
    isel = jnp.zeros(ci.shape, I32)
    jsel = jnp.zeros(ci.shape, I32)
    for a in range(PEER_TOPK):
        isel = jnp.where(a_idx == a, si[0][a:a + 1], isel)
        jsel = jnp.where(b_idx == a, si[1][a:a + 1], jsel)
    e = jnp.exp(top - top[0:1])
    ik_ref[...] = isel
    jk_ref[...] = jsel
    g_ref[...] = e / jnp.sum(e, axis=0, keepdims=True)


def peer_topk(q, subkeys, tm=256):
    N = q.shape[0]
    assert N % tm == 0
    half = subkeys.shape[-1]
    outs = pl.BlockSpec((PEER_TOPK, tm), lambda i, h: (h, i))
    shp = (PEER_HEADS * PEER_TOPK, N)
    return pl.pallas_call(
        _peer_topk_kernel,
        out_shape=[jax.ShapeDtypeStruct(shp, I32), jax.ShapeDtypeStruct(shp, I32), jax.ShapeDtypeStruct(shp, F32)],
        grid=(N // tm, PEER_HEADS),
        in_specs=[pl.BlockSpec((tm, 2 * half), lambda i, h: (i, h)),
                  pl.BlockSpec((1, 2, N_KEYS, half), lambda i, h: (h, 0, 0, 0))],
        out_specs=[outs, outs, outs],
        compiler_params=_cp(("parallel", "parallel")), name="peer_topk",
    )(q, subkeys)


def _peer_w_kernel(ik_ref, jk_ref, g_ref, w_ref, iks, jks, gs):
    iks[...] = ik_ref[...].T
    jks[...] = jk_ref[...].T
    gs[...] = g_ref[...].T
    hk = iks.shape[1]
    key = _iota((N_KEYS, hk), 0)

    def body(n, carry):
        a = jnp.where(key == iks[pl.ds(n, 1), :], gs[pl.ds(n, 1), :], 0.0).astype(CDT)
        b = jnp.where(key == jks[pl.ds(n, 1), :], 1.0, 0.0).astype(CDT)
        w_ref[n] = _nt(a, b).astype(w_ref.dtype)
        return carry

    lax.fori_loop(0, iks.shape[0], body, 0, unroll=8)


def peer_weights(ik, jk, g, tm=128):
    hk, N = ik.shape
    assert N % tm == 0
    blk = pl.BlockSpec((hk, tm), lambda i: (0, i))
    return pl.pallas_call(
        _peer_w_kernel,
        out_shape=jax.ShapeDtypeStruct((N, N_KEYS, N_KEYS), CDT),
        grid=(N // tm,), in_specs=[blk, blk, blk],
        out_specs=pl.BlockSpec((tm, N_KEYS, N_KEYS), lambda i: (i, 0, 0)),
        scratch_shapes=[pltpu.VMEM((tm, hk), I32), pltpu.VMEM((tm, hk), I32), pltpu.VMEM((tm, hk), F32)],
        compiler_params=_cp(("parallel",)), name="peer_weights",
    )(ik, jk, g)


def _peer_dense_kernel(x_ref, u_ref, v_ref, w_ref, o_ref, a_ref):
    e = pl.program_id(1)

    @pl.when(e == 0)
    def _():
        o_ref[...] = jnp.zeros(o_ref.shape, F32)
        a_ref[...] = jnp.zeros(a_ref.shape, a_ref.dtype)

    slot = e % 2
    prev = a_ref[1 - slot]
    h = _nt(x_ref[...], u_ref[...].astype(CDT))
    a_ref[slot] = (jax.nn.gelu(h) * w_ref[...].astype(F32)).astype(CDT)
    o_ref[...] += _dot(prev, v_ref[...].astype(CDT))


def peer_dense(x, u, v, w, layer, tm=1024, te=256):
    N, D = x.shape
    E = u.shape[1]
    tm = tm if N % tm == 0 else N
    n_e = E // te
    return pl.pallas_call(
        _peer_dense_kernel,
        out_shape=jax.ShapeDtypeStruct((N, D), F32),
        grid=(N // tm, n_e + 1),
        in_specs=[pl.BlockSpec((tm, D), lambda i, e: (i, 0)),
                  pl.BlockSpec((None, te, D), lambda i, e: (layer, jnp.minimum(e, n_e - 1), 0)),
                  pl.BlockSpec((None, te, D), lambda i, e: (layer, jnp.maximum(e - 1, 0), 0)),
                  pl.BlockSpec((tm, te), lambda i, e: (i, jnp.minimum(e, n_e - 1)))],
        out_specs=pl.BlockSpec((tm, D), lambda i, e: (i, 0)),
        scratch_shapes=[pltpu.VMEM((2, tm, te), CDT)],
        compiler_params=_cp(("parallel", "arbitrary")), name="peer_dense",
    )(x, u, v, w)


def peer(xn2d, wq, subkeys, u, v, layer):
    q = matmul(xn2d, wq)
    ik, jk, g = peer_topk(q, subkeys)
    w = peer_weights(ik, jk, g)
    return peer_dense(xn2d, u, v, w.reshape(w.shape[0], N_KEYS * N_KEYS), layer)


def _group_rows(mod, n_prompt_groups):
    w = mod.shape[1]
    return jnp.concatenate([jnp.broadcast_to(mod[0:1], (n_prompt_groups, w)), mod[1:]], axis=0)[:, None, :]


def nsa_mixer(xn, T, w_in, w_out, cmp_pe, cmp_w1, cmp_w2, rel_bias, pool, page_table, win_state):
    N = xn.shape[0]
    B, n_pages = page_table.shape
    n_pg = T // DEC_T
    nq, nkv = N_HEADS * HEAD_DIM, KV * HEAD_DIM
    n_pool, page = pool.shape[0], pool.shape[1]
    past = n_pages * page
    wb = win_state.shape[1]
    assert wb == WIN_A and page % CMP_STRIDE == 0 and page % SEL_BLOCK == 0 and page >= MAX_DISTANCE

    q = matmul(xn, w_in[:, :nq], out_dtype=CDT)
    rows = matmul(xn, w_in[:, nq:nq + 6 * nkv])
    w_gate = jnp.pad(w_in[:, nq + 6 * nkv:], ((0, 0), (0, LANE - N_BRANCH * N_HEADS)))
    gates = matmul(xn, w_gate, act_out="sigmoid")
    rows3 = rows.reshape(N // DEC_T, DEC_T, 6 * nkv)
    q_s = q[T:].astype(F32).reshape(B, DEC_T, nq)

    rows_bf_p = rows[:T].astype(CDT)
    kcvc = compress_prompt(rows[:T], cmp_pe, cmp_w1, cmp_w2)
    oc_p, sel_p = cmp_prompt(q, kcvc, rel_bias, n_q=T)
    tiles_sel = bias_tiles(rel_bias, 3, QB, QB, dstep=QB)
    os_p = flash_prompt(q, rows_bf_p, 2, 3, tiles_sel, n_q=T, max_back=None, sel=sel_p)
    tiles_w = bias_tiles(rel_bias, WIN_A // QB + 1, QB, QB, dstep=QB, window=WIN_A)
    ow_p = flash_prompt(q, rows_bf_p, 4, 5, tiles_w, n_q=T, max_back=WIN_A // QB)

    pool3 = pool.reshape(n_pool, page * N_SLAB, HEAD_DIM)
    pool_p = pool_compress(page_table, pool3, cmp_pe, cmp_w1)
    n_cmp_s = (past + DEC_T - CMP_LEN) // CMP_STRIDE + 1
    tile_c = bias_tiles(rel_bias, 1, DEC_T, past // CMP_STRIDE, delta0=past, cs=CMP_STRIDE, coff=CMP_LEN - 1,
                        ncols_valid=n_cmp_s)
    oc_s, pen_s = cmp_sample(pool_p, q_s, cmp_w2, tile_c, page=page)
    tiles_past = bias_tiles(rel_bias, 2, DEC_T, page, delta0=2 * page, dstep=-page)
    tile_new = bias_tiles(rel_bias, 1, DEC_T, page)
    os_s = sel_sample(page_table, pool3, q_s, pen_s, rows3, tiles_past, tile_new, g0=n_pg)
    tile_buf = bias_tiles(rel_bias, 1, DEC_T, wb, delta0=wb, window=WIN_A)
    tile_wnew = bias_tiles(rel_bias, 1, DEC_T, LANE, window=WIN_A)
    ow_s, new_win_s = win_sample(q_s, win_state.reshape(B, wb * 2 * KV, HEAD_DIM), rows3, 2, tile_buf, tile_wnew,
                                 g0=n_pg)

    def both(p, s):
        return jnp.concatenate([p, s.reshape(B * DEC_T, nq)], axis=0)

    o = combine(gates, both(oc_p, oc_s), both(os_p, os_s), both(ow_p, ow_s))
    return matmul(o, w_out), rows, new_win_s


def swa_mixer(xn, T, kvsh, w_q, sinks, w_out, rel_bias, state):
    N = xn.shape[0]
    B, wb = state.shape[0], state.shape[1]
    assert wb == WIN_B
    nq, nkv = N_HEADS * HEAD_DIM, KV * HEAD_DIM
    q = matmul(xn, w_q, out_dtype=CDT)
    tiles = bias_tiles(rel_bias, WIN_B // QB + 1, QB, QB, dstep=QB, window=WIN_B)
    o_p = flash_prompt(q, kvsh[:T].astype(CDT), 0, 1, tiles, n_q=T, max_back=WIN_B // QB, sinks=sinks)
    tile_buf = bias_tiles(rel_bias, 1, DEC_T, wb, delta0=wb, window=WIN_B)
    tile_new = bias_tiles(rel_bias, 1, DEC_T, LANE, window=WIN_B)
    o_s, new_state = win_sample(q[T:].astype(F32).reshape(B, DEC_T, nq), state.reshape(B, wb * 2 * KV, HEAD_DIM),
                                kvsh.reshape(N // DEC_T, DEC_T, 2 * nkv), 0, tile_buf, tile_new, g0=T // DEC_T,
                                sinks=sinks)
    o = jnp.concatenate([o_p, o_s.reshape(B * DEC_T, nq)], axis=0).astype(CDT)
    return matmul(o, w_out), new_state


def kernel(x_prompt, x_sample, cache_nsa_kv, state_nsa_win, state_swa_kv, page_table, c_prompt, c_sample,
           ada_w, ada_b, norm_mix, norm_ffn, nsa_w_in, nsa_w_out, nsa_cmp_pe, nsa_cmp_w1, nsa_cmp_w2,
           kv_ada_w, kv_ada_b, kv_norm, kv_w, swa_w_q, swa_sinks, swa_w_out, rel_bias,
           peer_wq, peer_subkeys, peer_u, peer_v, final_norm):
    assert x_prompt.shape[0] == 1 and x_sample.shape[1] == DEC_T
    T, D = x_prompt.shape[1], x_prompt.shape[2]
    B = x_sample.shape[0]
    assert T % QB == 0 and D == N_HEADS * HEAD_DIM
    n_pg = T // DEC_T
    NG = n_pg + B
    N = NG * DEC_T
    nq = N_HEADS * HEAD_DIM
    nkv = KV * HEAD_DIM
    wb_a, wb_b = state_nsa_win.shape[2], state_swa_kv.shape[1]
    assert state_nsa_win.shape[0] == 1 and cache_nsa_kv.shape[0] == 1 and ada_w.shape[0] == 2

    h = jnp.concatenate([x_prompt.reshape(T, D), x_sample.reshape(B * DEC_T, D)], axis=0).reshape(NG, DEC_T, D)
    c = jnp.concatenate([c_prompt, c_sample], axis=0)
    mods = [_group_rows(matmul(c, ada_w[i], ada_b[i], act_in="silu"), n_pg) for i in range(2)]
    mod_kv = _group_rows(matmul(c, kv_ada_w, kv_ada_b, act_in="silu"), n_pg)

    (xn,) = resmod(h, None, [(norm_mix[0], (mods[0], 0, 1))], [CDT], emit_h=False)
    out, rows, new_win_s = nsa_mixer(xn, T, nsa_w_in[0], nsa_w_out[0], nsa_cmp_pe[0], nsa_cmp_w1[0], nsa_cmp_w2[0],
                                     rel_bias, cache_nsa_kv[0], page_table, state_nsa_win[0])
    h, xn = resmod(h, (out.reshape(NG, DEC_T, D), (mods[0], 2)), [(norm_ffn[0], (mods[0], 3, 4))], [CDT], emit_h=True)
    ffn = peer(xn, peer_wq[0], peer_subkeys[0], peer_u, peer_v, 0)

    h, xn, xkv = resmod(h, (ffn.reshape(NG, DEC_T, D), (mods[0], 5)),
                        [(norm_mix[1], (mods[1], 0, 1)), (kv_norm, (mod_kv, 0, 1))], [CDT, CDT], emit_h=True)
    kvsh = matmul(xkv, kv_w)
    out, new_swa_s = swa_mixer(xn, T, kvsh, swa_w_q[0], swa_sinks[0], swa_w_out[0], rel_bias, state_swa_kv)
    h, xn = resmod(h, (out.reshape(NG, DEC_T, D), (mods[1], 2)), [(norm_ffn[1], (mods[1], 3, 4))], [CDT], emit_h=True)
    ffn = peer(xn, peer_wq[1], peer_subkeys[1], peer_u, peer_v, 1)
    (y2d,) = resmod(h, (ffn.reshape(NG, DEC_T, D), (mods[1], 5)), [(final_norm, None)], [F32], emit_h=False)

    rows_p, rows_s = rows[:T], rows[T:]
    return (y2d[:T].reshape(1, T, D),
            y2d[T:].reshape(B, DEC_T, D),
            rows_p[:, :4 * nkv].reshape(1, 1, T, 4, KV, HEAD_DIM),
            rows_s[:, :4 * nkv].reshape(1, B, DEC_T, 4, KV, HEAD_DIM),
            rows_p[T - min(WIN_A, T):, 4 * nkv:].reshape(1, 1, min(WIN_A, T), 2, KV, HEAD_DIM),
            new_win_s.reshape(1, B, wb_a, 2, KV, HEAD_DIM),
            kvsh[T - min(WIN_B, T):T].reshape(1, min(WIN_B, T), 2, KV, HEAD_DIM),
            new_swa_s.reshape(B, wb_b, 2, KV, HEAD_DIM))
```

```python
import functools
import math

import jax
import jax.numpy as jnp
from jax import lax
from jax.experimental import pallas as pl
from jax.experimental.pallas import tpu as pltpu

F32 = jnp.float32
I32 = jnp.int32
CDT = jnp.bfloat16

N_HEADS = 16
HEAD_DIM = 128
KV = 2
G = N_HEADS // KV
SCALE = HEAD_DIM ** -0.5
CMP_LEN = 32
CMP_STRIDE = 16
SEL_BLOCK = 64
SEL_TOPK = 16
N_LOCAL_FORCED = 2
WIN_A = 512
WIN_B = 128
N_BRANCH = 3
FORCED_SCORE = 1e9
NUM_BUCKETS = 32
MAX_EXACT = 16
MAX_DISTANCE = 128
PEER_HEADS = 8
N_KEYS = 128
PEER_TOPK = 16
EPS = 1e-6
SEL_SHIFT = SEL_BLOCK.bit_length() - 1
TOPK_SHIFT = PEER_TOPK.bit_length() - 1
QB = 128
DEC_T = 8
LANE = 128
NEG = -jnp.inf

VMEM_LIMIT = 56 * 1024 * 1024


def _cp(sem=None, vmem=VMEM_LIMIT):
    return pltpu.CompilerParams(dimension_semantics=sem, vmem_limit_bytes=vmem)


def _nt(a, b):
    return lax.dot_general(a, b, (((1,), (1,)), ((), ())), preferred_element_type=F32)


def _dot(a, b):
    return jnp.dot(a, b, preferred_element_type=F32)


def _iota(shape, axis):
    return lax.broadcasted_iota(I32, shape, axis)


def _mm_kernel(x_ref, w_ref, b_ref, o_ref, *, act_in, act_out):
    x = x_ref[...]
    if act_in == "silu":
        x = x.astype(F32)
        x = x * jax.nn.sigmoid(x)
    acc = _dot(x.astype(CDT), w_ref[...].astype(CDT)) + b_ref[...]
    if act_out == "sigmoid":
        acc = jax.nn.sigmoid(acc)
    o_ref[...] = acc.astype(o_ref.dtype)


def matmul(x, w, b=None, *, act_in=None, act_out=None, out_dtype=F32, tm=512, tn=512):
    M, K = x.shape
    N = w.shape[1]
    tm = tm if M % tm == 0 else M
    tn = tn if N % tn == 0 else N
    if b is None:
        b = jnp.zeros((N,), F32)
    b = b.reshape(1, N).astype(F32)
    return pl.pallas_call(
        functools.partial(_mm_kernel, act_in=act_in, act_out=act_out),
        out_shape=jax.ShapeDtypeStruct((M, N), out_dtype),
        grid=(N // tn, M // tm),
        in_specs=[pl.BlockSpec((tm, K), lambda j, i: (i, 0)),
                  pl.BlockSpec((K, tn), lambda j, i: (0, j)),
                  pl.BlockSpec((1, tn), lambda j, i: (0, j))],
        out_specs=pl.BlockSpec((tm, tn), lambda j, i: (i, j)),
        compiler_params=_cp(("parallel", "parallel")),
        name="matmul",
    )(x, w, b)


def _resmod_kernel(*refs, has_res, mod_flags, emit_h):
    it = iter(refs)
    h = next(it)[...]
    if has_res:
        o_ref, g_ref = next(it), next(it)
        h = h + g_ref[...] * o_ref[...]
    mods = []
    for has_mod in mod_flags:
        gain = next(it)
        mods.append((gain, next(it), next(it)) if has_mod else (gain, None, None))
    if emit_h:
        next(it)[...] = h
    y = h * lax.rsqrt(jnp.mean(h * h, axis=-1, keepdims=True) + EPS)
    for gain, sh, sc in mods:
        z = y * gain[...]
        if sh is not None:
            z = z * (1.0 + sc[...]) + sh[...]
        out = next(it)
        out[...] = z.reshape(out.shape).astype(out.dtype)


def resmod(h, res, mods, out_dtypes, *, emit_h, gb=16):
    NG, R, D = h.shape
    gb = gb if NG % gb == 0 else NG
    tok = pl.BlockSpec((gb, R, D), lambda i: (i, 0, 0))
    tok2 = pl.BlockSpec((gb * R, D), lambda i: (i, 0))

    def chunk(c):
        return pl.BlockSpec((gb, 1, D), lambda i, c=c: (i, 0, c))

    args, specs = [h], [tok]
    if res is not None:
        out, (garr, gc) = res
        args += [out, garr]
        specs += [tok, chunk(gc)]
    flags = []
    for gain, m in mods:
        args.append(gain.reshape(1, 1, D))
        specs.append(pl.BlockSpec((1, 1, D), lambda i: (0, 0, 0)))
        flags.append(m is not None)
        if m is not None:
            marr, shc, scc = m
            args += [marr, marr]
            specs += [chunk(shc), chunk(scc)]
    out_shape, out_specs = [], []
    if emit_h:
        out_shape.append(jax.ShapeDtypeStruct(h.shape, F32))
        out_specs.append(tok)
    for dt in out_dtypes:
        out_shape.append(jax.ShapeDtypeStruct((NG * R, D), dt))
        out_specs.append(tok2)
    return pl.pallas_call(
        functools.partial(_resmod_kernel, has_res=res is not None, mod_flags=tuple(flags), emit_h=emit_h),
        out_shape=out_shape, grid=(NG // gb,), in_specs=specs, out_specs=out_specs,
        compiler_params=_cp(("parallel",)), name="resmod",
    )(*args)


def _t5_bucket(dist):
    n = jnp.maximum(dist, 0)
    nf = jnp.maximum(n, 1).astype(F32)
    large = MAX_EXACT + (jnp.log(nf / MAX_EXACT) / math.log(MAX_DISTANCE / MAX_EXACT)
                         * (NUM_BUCKETS - MAX_EXACT)).astype(I32)
    large = jnp.minimum(large, NUM_BUCKETS - 1)
    return jnp.where(n < MAX_EXACT, n, large)


def _bias_lookup(bucket, rb_ref, h):
    out = jnp.full(bucket.shape, rb_ref[NUM_BUCKETS - 1, h], F32)
    for b in range(NUM_BUCKETS - 1):
        out = jnp.where(bucket == b, rb_ref[b, h], out)
    return out


def _bias_tile_kernel(rb_ref, o_ref, *, delta0, dstep, cs, coff, window, ncols_valid, keys_first):
    ti = pl.program_id(0)
    h = pl.program_id(1)
    shape = o_ref.shape[-2:]
    i = _iota(shape, 1 if keys_first else 0)
    j = _iota(shape, 0 if keys_first else 1)
    dist = delta0 + ti * dstep + i - (j * cs + coff)
    ok = dist >= 0
    if window is not None:
        ok = ok & (dist <= window)
    if ncols_valid is not None:
        ok = ok & (j < ncols_valid)
    o_ref[0, 0] = jnp.where(ok, _bias_lookup(_t5_bucket(dist), rb_ref, h), NEG)


def bias_tiles(rel_bias, n_tiles, rows, cols, *, delta0=0, dstep=0, cs=1, coff=0, window=None, ncols_valid=None,
               keys_first=False):
    return pl.pallas_call(
        functools.partial(_bias_tile_kernel, delta0=delta0, dstep=dstep, cs=cs, coff=coff, window=window,
                          ncols_valid=ncols_valid, keys_first=keys_first),
        out_shape=jax.ShapeDtypeStruct((n_tiles, N_HEADS, rows, cols), F32),
        grid=(n_tiles, N_HEADS),
        in_specs=[pl.BlockSpec(memory_space=pltpu.SMEM)],
        out_specs=pl.BlockSpec((1, 1, rows, cols), lambda t, h: (t, h, 0, 0)),
        compiler_params=_cp(("parallel", "parallel")), name="bias_tiles",
    )(rel_bias)


def _softmax_rows(s):
    m = jnp.max(s, axis=-1, keepdims=True)
    m = jnp.where(m == NEG, 0.0, m)
    e = jnp.exp(s - m)
    z = jnp.sum(e, axis=-1, keepdims=True)
    return e / jnp.where(z > 0, z, 1.0)


def _flash_step(s, v, m, l, acc):
    gg, r, kk = s.shape
    m_new = jnp.maximum(m, jnp.max(s, axis=-1, keepdims=True))
    m_safe = jnp.where(m_new == NEG, 0.0, m_new)
    alpha = jnp.exp(m - m_safe)
    p = jnp.exp(s - m_safe)
    l = alpha * l + jnp.sum(p, axis=-1, keepdims=True)
    pv = _dot(p.reshape(gg * r, kk).astype(CDT), v).reshape(gg, r, HEAD_DIM)
    return m_new, l, alpha * acc + pv


def _flash_finish(m, l, acc, sinks):
    if sinks is not None:
        m_f = jnp.maximum(m, sinks)
        a = jnp.exp(m - m_f)
        l = l * a + jnp.exp(sinks - m_f)
        acc = acc * a
    return acc / jnp.where(l > 0, l, 1.0)


def _head_rows(q, kv):
    return jnp.concatenate([q[:, (kv * G + g) * HEAD_DIM:(kv * G + g + 1) * HEAD_DIM] for g in range(G)],
                           axis=0).astype(CDT)


def _sink_col(sink_ref, kv):
    return jnp.concatenate([jnp.full((1, 1, 1), sink_ref[kv * G + g], F32) for g in range(G)], axis=0)


def _flash_prompt_kernel(*refs, n_tiles, max_back, has_sel, has_sinks, nsub):
    it = iter(refs)
    q_ref, k_ref, vt_ref, t_ref = next(it), next(it), next(it), next(it)
    sel_ref = next(it) if has_sel else None
    sink_ref = next(it) if has_sinks else None
    o_ref, m_ref, l_ref, acc_ref = next(it), next(it), next(it), next(it)
    qi = pl.program_id(0)
    kc = nsub * QB
    c_lo = 0 if max_back is None else jnp.maximum(qi - max_back, 0) // nsub
    c_hi = qi // nsub
    for kv in range(KV):
        cols = slice(kv * HEAD_DIM, (kv + 1) * HEAD_DIM)
        m_ref[...] = jnp.full(m_ref.shape, NEG, F32)
        l_ref[...] = jnp.zeros(l_ref.shape, F32)
        acc_ref[...] = jnp.zeros(acc_ref.shape, F32)
        qs = [q_ref[:, (kv * G + g) * HEAD_DIM:(kv * G + g + 1) * HEAD_DIM] for g in range(G)]

        def body(c, carry, kv=kv, qs=qs, cols=cols):
            start = pl.multiple_of(c * kc, kc)
            ks = k_ref[pl.ds(start, kc), cols]
            vts = vt_ref[cols, pl.ds(start, kc)]
            tidx = []
            for j in range(nsub):
                d = qi - (c * nsub + j)
                dead = d < 0 if max_back is None else (d < 0) | (d > max_back)
                tidx.append(jnp.where(dead, n_tiles, jnp.minimum(d, n_tiles - 1)))
            if has_sel:
                nj = sel_ref.shape[1]
                blk = (kc // SEL_BLOCK) * c + (_iota((kc, nj), 0) >> SEL_SHIFT)
                hit = _dot((_iota((kc, nj), 1) == blk).astype(CDT), sel_ref[kv])
                pen = jnp.where(hit > 0.5, 0.0, NEG)
            for g in range(G):
                h = kv * G + g
                s = _nt(ks, qs[g]) * SCALE + jnp.concatenate([t_ref[tidx[j], h] for j in range(nsub)], axis=0)
                if has_sel:
                    s = s + pen
                m = m_ref[g]
                m_new = jnp.maximum(m, jnp.max(s, axis=0, keepdims=True))
                m_safe = jnp.where(m_new == NEG, 0.0, m_new)
                alpha = jnp.exp(m - m_safe)
                p = jnp.exp(s - m_safe)
                m_ref[g] = m_new
                l_ref[g] = alpha * l_ref[g] + jnp.sum(p, axis=0, keepdims=True)
                acc_ref[g] = alpha * acc_ref[g] + _dot(vts, p.astype(CDT))
            return carry

        lax.fori_loop(c_lo, c_hi + 1, body, 0)
        for g in range(G):
            h = kv * G + g
            m, l, acc = m_ref[g], l_ref[g], acc_ref[g]
            if has_sinks:
                m_f = jnp.maximum(m, sink_ref[h])
                a = jnp.exp(m - m_f)
                l = l * a + jnp.exp(sink_ref[h] - m_f)
                acc = acc * a
            o_ref[:, h * HEAD_DIM:(h + 1) * HEAD_DIM] = (acc / jnp.where(l > 0, l, 1.0)).T


def flash_prompt(q, k_arr, kblk, v_t, tiles_t, *, n_q, max_back, sel_t=None, sinks=None, nsub=2):
    T = k_arr.shape[0]
    n_tiles = tiles_t.shape[0]
    assert T % (nsub * QB) == 0
    tiles_t = jnp.concatenate([tiles_t, jnp.full((1,) + tiles_t.shape[1:], NEG, F32)], axis=0)
    w = KV * HEAD_DIM
    args = [q, k_arr, v_t, tiles_t]
    specs = [pl.BlockSpec((QB, N_HEADS * HEAD_DIM), lambda i: (i, 0)),
             pl.BlockSpec((T, w), lambda i: (0, kblk)),
             pl.BlockSpec((w, T), lambda i: (0, 0)),
             pl.BlockSpec(tiles_t.shape, lambda i: (0, 0, 0, 0))]
    scratch = [pltpu.VMEM((G, 1, QB), F32), pltpu.VMEM((G, 1, QB), F32), pltpu.VMEM((G, HEAD_DIM, QB), F32)]
    if sel_t is not None:
        args.append(sel_t)
        specs.append(pl.BlockSpec((KV, sel_t.shape[1], QB), lambda i: (0, 0, i)))
    if sinks is not None:
        args.append(sinks)
        specs.append(pl.BlockSpec(memory_space=pltpu.SMEM))
    return pl.pallas_call(
        functools.partial(_flash_prompt_kernel, n_tiles=n_tiles, max_back=max_back, has_sel=sel_t is not None,
                          has_sinks=sinks is not None, nsub=nsub),
        out_shape=jax.ShapeDtypeStruct((n_q, N_HEADS * HEAD_DIM), F32),
        grid=(n_q // QB,), in_specs=specs,
        out_specs=pl.BlockSpec((QB, N_HEADS * HEAD_DIM), lambda i: (i, 0)),
        scratch_shapes=scratch,
        compiler_params=_cp(("parallel",)), name="flash_prompt",
    )(*args)


def _compress_rows(x, pe_ref, w1_ref, acc_ref, first):
    @pl.when(first)
    def _():
        acc_ref[...] = jnp.zeros(acc_ref.shape, F32)

    for gi in range(2 * KV):
        comp = gi // KV
        xg = x[:, gi * HEAD_DIM:(gi + 1) * HEAD_DIM]
        for a in range(2):
            xa = (xg + pe_ref[comp, a, 0]).astype(CDT)
            col = slice((gi * 2 + a) * HEAD_DIM, (gi * 2 + a + 1) * HEAD_DIM)
            acc_ref[:, col] += _dot(xa, w1_ref[comp, a, 0].astype(CDT))


def _compress_finish(p, w2_ref, gi):
    c = p.shape[0]
    pa = p[:, (gi * 2) * HEAD_DIM:(gi * 2 + 1) * HEAD_DIM]
    pb = p[:, (gi * 2 + 1) * HEAD_DIM:(gi * 2 + 2) * HEAD_DIM]
    h1 = jax.nn.gelu(pa + pltpu.roll(pb, c - 1, 0))
    return _dot(h1.astype(CDT), w2_ref[gi // KV].astype(CDT))


def _compress_prompt_kernel(x_ref, pe_ref, w1_ref, w2_ref, o_ref, acc_ref):
    b = pl.program_id(0)
    _compress_rows(x_ref[...], pe_ref, w1_ref, acc_ref, b == 0)

    @pl.when(b == CMP_STRIDE - 1)
    def _():
        p = acc_ref[...]
        for gi in range(2 * KV):
            o_ref[gi] = _compress_finish(p, w2_ref, gi)


def _cmp_weights(pe, w1):
    pe5 = pe.reshape(2, 2, CMP_STRIDE, 1, HEAD_DIM)
    w15 = w1.reshape(2, 2, CMP_STRIDE, HEAD_DIM, HEAD_DIM)
    return pe5, w15


def compress_prompt(rows, pe, w1, w2):
    T = rows.shape[0]
    C = T // CMP_STRIDE
    x = rows.reshape(C, CMP_STRIDE * rows.shape[1])
    nblk = rows.shape[1] // (4 * HEAD_DIM)
    pe5, w15 = _cmp_weights(pe, w1)
    return pl.pallas_call(
        _compress_prompt_kernel,
        out_shape=jax.ShapeDtypeStruct((2 * KV, C, HEAD_DIM), F32),
        grid=(CMP_STRIDE,),
        in_specs=[pl.BlockSpec((C, 4 * HEAD_DIM), lambda b: (0, b * nblk)),
                  pl.BlockSpec((2, 2, 1, 1, HEAD_DIM), lambda b: (0, 0, b, 0, 0)),
                  pl.BlockSpec((2, 2, 1, HEAD_DIM, HEAD_DIM), lambda b: (0, 0, b, 0, 0)),
                  pl.BlockSpec((2, HEAD_DIM, HEAD_DIM), lambda b: (0, 0, 0))],
        out_specs=pl.BlockSpec((2 * KV, C, HEAD_DIM), lambda b: (0, 0, 0)),
        scratch_shapes=[pltpu.VMEM((C, 8 * HEAD_DIM), F32)],
        compiler_params=_cp(("arbitrary",)), name="compress_prompt",
    )(x, pe5, w15, w2)


def _overlap(n_idx, j_idx, n_cmp, n_slc):
    return ((n_idx * CMP_STRIDE <= j_idx * SEL_BLOCK + SEL_BLOCK - 1)
            & (n_idx * CMP_STRIDE + CMP_LEN - 1 >= j_idx * SEL_BLOCK)
            & (n_idx < n_cmp) & (j_idx < n_slc))


def _split_hi_lo(x):
    hi = x.astype(CDT)
    return hi, (x - hi.astype(F32)).astype(CDT)


def _sel_scores(imp, j_idx, t, n_slc):
    valid = (j_idx * SEL_BLOCK <= t) & (j_idx < n_slc)
    cur = t >> SEL_SHIFT
    forced = (j_idx == 0) | ((j_idx <= cur) & (j_idx > cur - N_LOCAL_FORCED))
    return jnp.where(valid, jnp.where(forced, FORCED_SCORE, imp), NEG)


def _topk_mask(score, idx, axis, k):
    n = score.shape[axis]
    sel = jnp.zeros(score.shape, F32)
    for _ in range(k):
        m = jnp.max(score, axis=axis, keepdims=True)
        first = jnp.min(jnp.where(score == m, idx, n), axis=axis, keepdims=True)
        hit = idx == first
        sel = jnp.where(hit & (m > NEG), 1.0, sel)
        score = jnp.where(hit, NEG, score)
    return sel


def _cmp_prompt_kernel(q_ref, kc_ref, rb_ref, oc_ref, sel_ref, *, n_cmp, n_slc, k_top):
    qi = pl.program_id(0)
    t0 = qi * QB
    C = kc_ref.shape[1]
    J = sel_ref.shape[1]
    n_idx = _iota((QB, C), 1)
    dist = t0 + _iota((QB, C), 0) - (n_idx * CMP_STRIDE + CMP_LEN - 1)
    ok = (dist >= 0) & (n_idx < n_cmp)
    bucket = _t5_bucket(dist)
    ov_t = _overlap(_iota((J, C), 1), _iota((J, C), 0), n_cmp, n_slc).astype(CDT)
    j_t = _iota((J, QB), 0)
    t_t = t0 + _iota((J, QB), 1)
    q = q_ref[...]
    for kv in range(KV):
        q2 = _head_rows(q, kv)
        s = _nt(q2, kc_ref[kv].astype(CDT)).reshape(G, QB, C) * SCALE
        bias = jnp.stack([_bias_lookup(bucket, rb_ref, kv * G + g) for g in range(G)], axis=0)
        p = _softmax_rows(jnp.where(ok[None], s + bias, NEG))
        o = _dot(p.reshape(G * QB, C).astype(CDT), kc_ref[KV + kv].astype(CDT)).reshape(G, QB, HEAD_DIM)
        for g in range(G):
            oc_ref[:, (kv * G + g) * HEAD_DIM:(kv * G + g + 1) * HEAD_DIM] = o[g]
        hi, lo = _split_hi_lo(jnp.sum(p, axis=0))
        imp_t = _nt(ov_t, hi) + _nt(ov_t, lo)
        sel_t = _topk_mask(_sel_scores(imp_t, j_t, t_t, n_slc), j_t, 0, k_top)
        sel_ref[kv] = sel_t.astype(sel_ref.dtype)


def cmp_prompt(q, kcvc, rel_bias, *, n_q):
    C = kcvc.shape[1]
    n_cmp = (n_q - CMP_LEN) // CMP_STRIDE + 1
    n_slc = -(-n_q // SEL_BLOCK)
    return pl.pallas_call(
        functools.partial(_cmp_prompt_kernel, n_cmp=n_cmp, n_slc=n_slc, k_top=min(SEL_TOPK, n_slc)),
        out_shape=[jax.ShapeDtypeStruct((n_q, N_HEADS * HEAD_DIM), F32),
                   jax.ShapeDtypeStruct((KV, n_slc, n_q), CDT)],
        grid=(n_q // QB,),
        in_specs=[pl.BlockSpec((QB, N_HEADS * HEAD_DIM), lambda i: (i, 0)),
                  pl.BlockSpec(kcvc.shape, lambda i: (0, 0, 0)),
                  pl.BlockSpec(memory_space=pltpu.SMEM)],
        out_specs=[pl.BlockSpec((QB, N_HEADS * HEAD_DIM), lambda i: (i, 0)),
                   pl.BlockSpec((KV, n_slc, QB), lambda i: (0, 0, i))],
        compiler_params=_cp(("parallel",)), name="cmp_prompt",
    )(q, kcvc, rel_bias)


def _combine_kernel(g_ref, c_ref, s_ref, w_ref, o_ref):
    gates = g_ref[...]
    for h in range(N_HEADS):
        col = slice(h * HEAD_DIM, (h + 1) * HEAD_DIM)
        o = (gates[:, N_BRANCH * h:N_BRANCH * h + 1] * c_ref[:, col]
             + gates[:, N_BRANCH * h + 1:N_BRANCH * h + 2] * s_ref[:, col]
             + gates[:, N_BRANCH * h + 2:N_BRANCH * h + 3] * w_ref[:, col])
        o_ref[:, col] = o.astype(o_ref.dtype)


def combine(gates, oc, os_, ow, tm=256):
    N, D = oc.shape
    assert N % tm == 0
    tok = pl.BlockSpec((tm, D), lambda i: (i, 0))
    return pl.pallas_call(
        _combine_kernel, out_shape=jax.ShapeDtypeStruct((N, D), CDT), grid=(N // tm,),
        in_specs=[pl.BlockSpec((tm, gates.shape[1]), lambda i: (i, 0)), tok, tok, tok], out_specs=tok,
        compiler_params=_cp(("parallel",)), name="nsa_combine",
    )(gates, oc, os_, ow)


N_SLAB = 4 * KV


def _page_slab(p_ref, slab, row0, n, step):
    return p_ref[0, pl.ds(row0 * N_SLAB + slab, n, stride=step * N_SLAB), :]


def _pool_cmp_kernel(pt_ref, *refs, n_ops, cpp):
    p_refs = refs[:n_ops]
    pe_ref, w1_ref, o_ref = refs[n_ops:]
    nc = n_ops * cpp
    for comp in range(2):
        x = jnp.concatenate(
            [jnp.concatenate([jnp.concatenate([_page_slab(p, comp * KV + kv, b, cpp, CMP_STRIDE) for p in p_refs],
                                              axis=0) for b in range(CMP_STRIDE)], axis=1) for kv in range(KV)],
            axis=0)
        for a in range(2):
            d = _dot((x + pe_ref[comp, a]).astype(CDT), w1_ref[comp, a])
            for kv in range(KV):
                col = ((comp * KV + kv) * 2 + a) * HEAD_DIM
                o_ref[:, col:col + HEAD_DIM] = d[kv * nc:(kv + 1) * nc]


def pool_compress(page_table, pool3, pe, w1, n_ops=16):
    B, n_pages = page_table.shape
    page = pool3.shape[1] // N_SLAB
    cpp = page // CMP_STRIDE
    n_ops = n_ops if n_pages % n_ops == 0 else n_pages
    ng = n_pages // n_ops
    half = CMP_STRIDE * HEAD_DIM
    pe4 = pe.reshape(2, 2, 1, half)
    w14 = w1.reshape(2, 2, half, HEAD_DIM).astype(CDT)
    p_specs = [pl.BlockSpec((1, page * N_SLAB, HEAD_DIM), lambda b, g, pt, r=r: (pt[b, g * n_ops + r], 0, 0))
               for r in range(n_ops)]
    grid_spec = pltpu.PrefetchScalarGridSpec(
        num_scalar_prefetch=1, grid=(B, ng),
        in_specs=p_specs + [pl.BlockSpec(pe4.shape, lambda b, g, pt: (0, 0, 0, 0)),
                            pl.BlockSpec(w14.shape, lambda b, g, pt: (0, 0, 0, 0))],
        out_specs=pl.BlockSpec((n_ops * cpp, 8 * HEAD_DIM), lambda b, g, pt: (b * ng + g, 0)))
    return pl.pallas_call(
        functools.partial(_pool_cmp_kernel, n_ops=n_ops, cpp=cpp),
        out_shape=jax.ShapeDtypeStruct((B * n_pages * cpp, 8 * HEAD_DIM), F32),
        grid_spec=grid_spec, compiler_params=_cp(("parallel", "arbitrary")), name="pool_compress",
    )(page_table, *([pool3] * n_ops), pe4, w14)


def _topk_rank_mask(score, k, n_valid):
    nr, nj = score.shape
    jr = -(-n_valid // 8) * 8
    st = jnp.concatenate([score, jnp.full((LANE - nr, nj), NEG, F32)], axis=0).T
    idx_c = _iota((jr, nj), 0)
    idx_r = _iota((jr, nj), 1)
    rows = []
    for r in range(nr):
        col = st[:jr, r:r + 1]
        row = score[r:r + 1, :]
        beats = (col > row) | ((col == row) & (idx_c < idx_r))
        rank = jnp.sum(jnp.where(beats, 1.0, 0.0), axis=0, keepdims=True)
        rows.append(jnp.where((rank < k) & (row > NEG), 1.0, 0.0))
    return jnp.concatenate(rows, axis=0)


def _cmp_sample_kernel(p_ref, q_ref, w2_ref, t_ref, e_ref, oc_ref, pen_ref, *, n_cmp, n_slc, k_top, q_pos0):
    p = p_ref[...]
    C = p.shape[0]
    J = e_ref.shape[0]
    q = q_ref[0]
    ov = _overlap(_iota((C, J), 0), _iota((C, J), 1), n_cmp, n_slc).astype(CDT)
    j_idx = _iota((DEC_T, J), 1)
    t = q_pos0 + _iota((DEC_T, J), 0)
    scores = []
    for kv in range(KV):
        kc = _compress_finish(p, w2_ref, kv).astype(CDT)
        vc = _compress_finish(p, w2_ref, KV + kv).astype(CDT)
        q2 = _head_rows(q, kv)
        s = _nt(q2, kc).reshape(G, DEC_T, C) * SCALE + t_ref[0, kv * G:(kv + 1) * G]
        pr = _softmax_rows(s)
        o = _dot(pr.reshape(G * DEC_T, C).astype(CDT), vc)
        for g in range(G):
            oc_ref[0, :, (kv * G + g) * HEAD_DIM:(kv * G + g + 1) * HEAD_DIM] = o[g * DEC_T:(g + 1) * DEC_T]
        hi, lo = _split_hi_lo(jnp.sum(pr, axis=0))
        imp = _dot(hi, ov) + _dot(lo, ov)
        scores.append(_sel_scores(imp, j_idx, t, n_slc))
    sel = _topk_rank_mask(jnp.concatenate(scores, axis=0), k_top, n_slc)
    pen_ref[0] = jnp.where(_dot(sel.astype(CDT), e_ref[...]) > 0.5, 0.0, NEG)


def cmp_sample(pool_p, q3, w2, tile_c, *, page):
    B = q3.shape[0]
    C = pool_p.shape[0] // B
    past = C * CMP_STRIDE
    L = past + DEC_T
    n_cmp = (L - CMP_LEN) // CMP_STRIDE + 1
    n_slc = -(-L // SEL_BLOCK)
    jp = -(-n_slc // LANE) * LANE
    D = N_HEADS * HEAD_DIM
    nk = past + page
    e_mat = (jnp.arange(jp, dtype=I32)[:, None] == jnp.arange(nk, dtype=I32)[None, :] // SEL_BLOCK).astype(CDT)
    return pl.pallas_call(
        functools.partial(_cmp_sample_kernel, n_cmp=n_cmp, n_slc=n_slc, k_top=min(SEL_TOPK, n_slc), q_pos0=past),
        out_shape=[jax.ShapeDtypeStruct((B, DEC_T, D), F32), jax.ShapeDtypeStruct((B, KV * DEC_T, nk), F32)],
        grid=(B,),
        in_specs=[pl.BlockSpec((C, 8 * HEAD_DIM), lambda b: (b, 0)),
                  pl.BlockSpec((1, DEC_T, D), lambda b: (b, 0, 0)),
                  pl.BlockSpec(w2.shape, lambda b: (0, 0, 0)),
                  pl.BlockSpec(tile_c.shape, lambda b: (0, 0, 0, 0)),
                  pl.BlockSpec((jp, nk), lambda b: (0, 0))],
        out_specs=[pl.BlockSpec((1, DEC_T, D), lambda b: (b, 0, 0)),
                   pl.BlockSpec((1, KV * DEC_T, nk), lambda b: (b, 0, 0))],
        compiler_params=_cp(("parallel",)), name="cmp_sample",
    )(pool_p, q3, w2, tile_c, e_mat)


def _sel_sample_kernel(pt_ref, *refs, n_ops, n_pages):
    p_refs = refs[:n_ops]
    q_ref, pen_ref, new_ref, tp_ref, tn_ref, o_ref, m_ref, l_ref, acc_ref = refs[n_ops:]
    gstep = pl.program_id(1)
    page = p_refs[0].shape[1] // N_SLAB
    nk = n_ops * page

    @pl.when(gstep == 0)
    def _():
        m_ref[...] = jnp.full(m_ref.shape, NEG, F32)
        l_ref[...] = jnp.zeros(l_ref.shape, F32)
        acc_ref[...] = jnp.zeros(acc_ref.shape, F32)

    q = q_ref[0]

    def update(kv, k, v, bias, pen):
        s = _nt(_head_rows(q, kv), k).reshape(G, DEC_T, k.shape[0]) * SCALE + bias + pen[None]
        m, l, acc = _flash_step(s, v, m_ref[kv], l_ref[kv], acc_ref[kv])
        m_ref[kv] = m
        l_ref[kv] = l
        acc_ref[kv] = acc

    for kv in range(KV):
        k = jnp.concatenate([_page_slab(p, 2 * KV + kv, 0, page, 1) for p in p_refs], axis=0).astype(CDT)
        v = jnp.concatenate([_page_slab(p, 3 * KV + kv, 0, page, 1) for p in p_refs], axis=0).astype(CDT)
        bias = jnp.concatenate(
            [tp_ref[jnp.where(gstep * n_ops + r == n_pages - 1, 1, 0), kv * G:(kv + 1) * G] for r in range(n_ops)],
            axis=-1)
        pen = pen_ref[0, kv * DEC_T:(kv + 1) * DEC_T, pl.ds(pl.multiple_of(gstep * nk, nk), nk)]
        update(kv, k, v, bias, pen)

    @pl.when(gstep == pl.num_programs(1) - 1)
    def _():
        new = new_ref[0]
        pad = jnp.zeros((page - DEC_T, HEAD_DIM), F32)
        past = n_pages * page
        for kv in range(KV):
            k = jnp.concatenate([new[:, kv * HEAD_DIM:(kv + 1) * HEAD_DIM], pad], axis=0).astype(CDT)
            v = jnp.concatenate([new[:, (KV + kv) * HEAD_DIM:(KV + kv + 1) * HEAD_DIM], pad], axis=0).astype(CDT)
            update(kv, k, v, tn_ref[0, kv * G:(kv + 1) * G], pen_ref[0, kv * DEC_T:(kv + 1) * DEC_T, past:past + page])
            o = _flash_finish(m_ref[kv], l_ref[kv], acc_ref[kv], None)
            for g in range(G):
                o_ref[0, :, (kv * G + g) * HEAD_DIM:(kv * G + g + 1) * HEAD_DIM] = o[g]


def sel_sample(page_table, pool3, q3, pen, rows3, tiles_past, tile_new, *, g0, n_ops=16):
    B, n_pages = page_table.shape
    page = pool3.shape[1] // N_SLAB
    n_ops = n_ops if n_pages % n_ops == 0 else n_pages
    D = N_HEADS * HEAD_DIM
    w = 2 * KV * HEAD_DIM
    p_specs = [pl.BlockSpec((1, page * N_SLAB, HEAD_DIM), lambda b, g, pt, r=r: (pt[b, g * n_ops + r], 0, 0))
               for r in range(n_ops)]
    grid_spec = pltpu.PrefetchScalarGridSpec(
        num_scalar_prefetch=1, grid=(B, n_pages // n_ops),
        in_specs=p_specs + [pl.BlockSpec((1, DEC_T, D), lambda b, g, pt: (b, 0, 0)),
                            pl.BlockSpec((1,) + pen.shape[1:], lambda b, g, pt: (b, 0, 0)),
                            pl.BlockSpec((1, DEC_T, w), lambda b, g, pt: (g0 + b, 0, 1)),
                            pl.BlockSpec(tiles_past.shape, lambda b, g, pt: (0, 0, 0, 0)),
                            pl.BlockSpec(tile_new.shape, lambda b, g, pt: (0, 0, 0, 0))],
        out_specs=pl.BlockSpec((1, DEC_T, D), lambda b, g, pt: (b, 0, 0)),
        scratch_shapes=[pltpu.VMEM((KV, G, DEC_T, 1), F32), pltpu.VMEM((KV, G, DEC_T, 1), F32),
                        pltpu.VMEM((KV, G, DEC_T, HEAD_DIM), F32)])
    return pl.pallas_call(
        functools.partial(_sel_sample_kernel, n_ops=n_ops, n_pages=n_pages),
        out_shape=jax.ShapeDtypeStruct((B, DEC_T, D), F32),
        grid_spec=grid_spec, compiler_params=_cp(("parallel", "arbitrary")), name="sel_sample",
    )(page_table, *([pool3] * n_ops), q3, pen, rows3, tiles_past, tile_new)


def _win_sample_kernel(*refs, has_sinks):
    it = iter(refs)
    q_ref, buf_ref, new_ref, tb_ref, tn_ref = next(it), next(it), next(it), next(it), next(it)
    sink_ref = next(it) if has_sinks else None
    o_ref, st_ref = next(it), next(it)
    q = q_ref[0]
    new = new_ref[0]
    ns = 2 * KV
    wb = buf_ref.shape[1] // ns
    npad = tn_ref.shape[-1]
    pad = jnp.zeros((npad - DEC_T, HEAD_DIM), F32)
    for kv in range(KV):
        kc = slice(kv * HEAD_DIM, (kv + 1) * HEAD_DIM)
        vc = slice((KV + kv) * HEAD_DIM, (KV + kv + 1) * HEAD_DIM)
        k = jnp.concatenate([buf_ref[0, pl.ds(kv, wb, stride=ns), :], new[:, kc], pad], axis=0).astype(CDT)
        v = jnp.concatenate([buf_ref[0, pl.ds(KV + kv, wb, stride=ns), :], new[:, vc], pad], axis=0).astype(CDT)
        bias = jnp.concatenate([tb_ref[0, kv * G:(kv + 1) * G], tn_ref[0, kv * G:(kv + 1) * G]], axis=-1)
        s = _nt(_head_rows(q, kv), k).reshape(G, DEC_T, wb + npad) * SCALE + bias
        init = (jnp.full((G, DEC_T, 1), NEG, F32), jnp.zeros((G, DEC_T, 1), F32), jnp.zeros((G, DEC_T, HEAD_DIM), F32))
        m, l, acc = _flash_step(s, v, *init)
        o = _flash_finish(m, l, acc, _sink_col(sink_ref, kv) if has_sinks else None)
        for g in range(G):
            o_ref[0, :, (kv * G + g) * HEAD_DIM:(kv * G + g + 1) * HEAD_DIM] = o[g]
    st_ref[0, :(wb - DEC_T) * ns, :] = buf_ref[0, DEC_T * ns:, :]
    for s in range(ns):
        st_ref[0, pl.ds((wb - DEC_T) * ns + s, DEC_T, stride=ns), :] = new[:, s * HEAD_DIM:(s + 1) * HEAD_DIM]


def win_sample(q3, buf, new3, new_blk, tile_buf, tile_new, *, g0, sinks=None):
    B, nrow = buf.shape[0], buf.shape[1]
    w = 2 * KV * HEAD_DIM
    D = N_HEADS * HEAD_DIM
    args = [q3, buf, new3, tile_buf, tile_new]
    specs = [pl.BlockSpec((1, DEC_T, D), lambda b: (b, 0, 0)),
             pl.BlockSpec((1, nrow, HEAD_DIM), lambda b: (b, 0, 0)),
             pl.BlockSpec((1, DEC_T, w), lambda b: (g0 + b, 0, new_blk)),
             pl.BlockSpec(tile_buf.shape, lambda b: (0, 0, 0, 0)),
             pl.BlockSpec(tile_new.shape, lambda b: (0, 0, 0, 0))]
    if sinks is not None:
        args.append(sinks)
        specs.append(pl.BlockSpec(memory_space=pltpu.SMEM))
    return pl.pallas_call(
        functools.partial(_win_sample_kernel, has_sinks=sinks is not None),
        out_shape=[jax.ShapeDtypeStruct((B, DEC_T, D), F32), jax.ShapeDtypeStruct(buf.shape, F32)],
        grid=(B,), in_specs=specs,
        out_specs=[pl.BlockSpec((1, DEC_T, D), lambda b: (b, 0, 0)),
                   pl.BlockSpec((1, nrow, HEAD_DIM), lambda b: (b, 0, 0))],
        compiler_params=_cp(("parallel",)), name="win_sample",
    )(*args)


def _topk_rows(s, k):
    r = s.shape[0]
    idx = _iota(s.shape, 0)
    vals, ids = [], []
    for _ in range(k):
        m = jnp.max(s, axis=0, keepdims=True)
        first = jnp.min(jnp.where(s == m, idx, r), axis=0, keepdims=True)
        vals.append(m)
        ids.append(first)
        s = jnp.where(idx == first, NEG, s)
    return jnp.concatenate(vals, axis=0), jnp.concatenate(ids, axis=0)


def _peer_topk_kernel(q_ref, sk_ref, ik_ref, jk_ref, g_ref):
    half = sk_ref.shape[-1]
    sv, si = [], []
    for c in range(2):
        s_t = _nt(sk_ref[0, c].astype(CDT), q_ref[:, c * half:(c + 1) * half].astype(CDT))
        v, i = _topk_rows(s_t, PEER_TOPK)
        sv.append(v)
        si.append(i)
    cand = jnp.concatenate([sv[0][a:a + 1] + sv[1] for a in range(PEER_TOPK)], axis=0)
    top, ci = _topk_rows(cand, PEER_TOPK)
    a_idx = ci >> TOPK_SHIFT
    b_idx = ci & (PEER_TOPK - 1)
    isel = jnp.zeros(ci.shape, I32)
    jsel = jnp.zeros(ci.shape, I32)
    for a in range(PEER_TOPK):
        isel = jnp.where(a_idx == a, si[0][a:a + 1], isel)
        jsel = jnp.where(b_idx == a, si[1][a:a + 1], jsel)
    e = jnp.exp(top - top[0:1])
    ik_ref[...] = isel
    jk_ref[...] = jsel
    g_ref[...] = e / jnp.sum(e, axis=0, keepdims=True)


def peer_topk(q, subkeys, tm=256):
    N = q.shape[0]
    assert N % tm == 0
    half = subkeys.shape[-1]
    outs = pl.BlockSpec((PEER_TOPK, tm), lambda i, h: (h, i))
    shp = (PEER_HEADS * PEER_TOPK, N)
    return pl.pallas_call(
        _peer_topk_kernel,
        out_shape=[jax.ShapeDtypeStruct(shp, I32), jax.ShapeDtypeStruct(shp, I32), jax.ShapeDtypeStruct(shp, F32)],
        grid=(N // tm, PEER_HEADS),
        in_specs=[pl.BlockSpec((tm, 2 * half), lambda i, h: (i, h)),
                  pl.BlockSpec((1, 2, N_KEYS, half), lambda i, h: (h, 0, 0, 0))],
        out_specs=[outs, outs, outs],
        compiler_params=_cp(("parallel", "parallel")), name="peer_topk",
    )(q, subkeys)


def _peer_w_kernel(ik_ref, jk_ref, g_ref, w_ref, iks, jks, gs):
    iks[...] = ik_ref[...].T
    jks[...] = jk_ref[...].T
    gs[...] = g_ref[...].T
    hk = iks.shape[1]
    key = _iota((N_KEYS, hk), 0)

    def body(n, carry):
        a = jnp.where(key == iks[pl.ds(n, 1), :], gs[pl.ds(n, 1), :], 0.0).astype(CDT)
        b = jnp.where(key == jks[pl.ds(n, 1), :], 1.0, 0.0).astype(CDT)
        w_ref[n] = _nt(a, b).astype(w_ref.dtype)
        return carry

    lax.fori_loop(0, iks.shape[0], body, 0, unroll=8)


def peer_weights(ik, jk, g, tm=128):
    hk, N = ik.shape
    assert N % tm == 0
    blk = pl.BlockSpec((hk, tm), lambda i: (0, i))
    return pl.pallas_call(
        _peer_w_kernel,
        out_shape=jax.ShapeDtypeStruct((N, N_KEYS, N_KEYS), CDT),
        grid=(N // tm,), in_specs=[blk, blk, blk],
        out_specs=pl.BlockSpec((tm, N_KEYS, N_KEYS), lambda i: (i, 0, 0)),
        scratch_shapes=[pltpu.VMEM((tm, hk), I32), pltpu.VMEM((tm, hk), I32), pltpu.VMEM((tm, hk), F32)],
        compiler_params=_cp(("parallel",)), name="peer_weights",
    )(ik, jk, g)


def _peer_dense_kernel(x_ref, u_ref, v_ref, w_ref, o_ref, a_ref):
    e = pl.program_id(1)

    @pl.when(e == 0)
    def _():
        o_ref[...] = jnp.zeros(o_ref.shape, F32)
        a_ref[...] = jnp.zeros(a_ref.shape, a_ref.dtype)

    slot = e % 2
    prev = a_ref[1 - slot]
    h = _nt(x_ref[...], u_ref[...].astype(CDT))
    a_ref[slot] = (jax.nn.gelu(h) * w_ref[...].astype(F32)).astype(CDT)
    o_ref[...] += _dot(prev, v_ref[...].astype(CDT))


def peer_dense(x, u, v, w, layer, tm=1024, te=512):
    N, D = x.shape
    E = u.shape[1]
    tm = tm if N % tm == 0 else N
    n_e = E // te
    return pl.pallas_call(
        _peer_dense_kernel,
        out_shape=jax.ShapeDtypeStruct((N, D), F32),
        grid=(N // tm, n_e + 1),
        in_specs=[pl.BlockSpec((tm, D), lambda i, e: (i, 0)),
                  pl.BlockSpec((None, te, D), lambda i, e: (layer, jnp.minimum(e, n_e - 1), 0)),
                  pl.BlockSpec((None, te, D), lambda i, e: (layer, jnp.maximum(e - 1, 0), 0)),
                  pl.BlockSpec((tm, te), lambda i, e: (i, jnp.minimum(e, n_e - 1)))],
        out_specs=pl.BlockSpec((tm, D), lambda i, e: (i, 0)),
        scratch_shapes=[pltpu.VMEM((2, tm, te), CDT)],
        compiler_params=_cp(("parallel", "arbitrary")), name="peer_dense",
    )(x, u, v, w)


def peer(xn2d, wq, subkeys, u, v, layer):
    q = matmul(xn2d, wq)
    ik, jk, g = peer_topk(q, subkeys)
    w = peer_weights(ik, jk, g)
    return peer_dense(xn2d, u, v, w.reshape(w.shape[0], N_KEYS * N_KEYS), layer)


def _group_rows(mod, n_prompt_groups):
    w = mod.shape[1]
    return jnp.concatenate([jnp.broadcast_to(mod[0:1], (n_prompt_groups, w)), mod[1:]], axis=0)[:, None, :]


def nsa_mixer(xn, T, w_in, w_out, cmp_pe, cmp_w1, cmp_w2, rel_bias, pool, page_table, win_state):
    N = xn.shape[0]
    B, n_pages = page_table.shape
    n_pg = T // DEC_T
    nq, nkv = N_HEADS * HEAD_DIM, KV * HEAD_DIM
    n_pool, page = pool.shape[0], pool.shape[1]
    past = n_pages * page
    wb = win_state.shape[1]
    assert wb == WIN_A and page % CMP_STRIDE == 0 and page % SEL_BLOCK == 0 and page >= MAX_DISTANCE

    q = matmul(xn, w_in[:, :nq], out_dtype=CDT)
    rows = matmul(xn, w_in[:, nq:nq + 6 * nkv])
    w_gate = jnp.pad(w_in[:, nq + 6 * nkv:], ((0, 0), (0, LANE - N_BRANCH * N_HEADS)))
    gates = matmul(xn, w_gate, act_out="sigmoid")
    rows3 = rows.reshape(N // DEC_T, DEC_T, 6 * nkv)
    q_s = q[T:].astype(F32).reshape(B, DEC_T, nq)

    rows_bf_p = rows[:T].astype(CDT)
    kcvc = compress_prompt(rows[:T], cmp_pe, cmp_w1, cmp_w2)
    oc_p, sel_p = cmp_prompt(q, kcvc, rel_bias, n_q=T)
    tiles_sel = bias_tiles(rel_bias, 3, QB, QB, dstep=QB, keys_first=True)
    os_p = flash_prompt(q, rows_bf_p, 2, rows_bf_p[:, 3 * nkv:4 * nkv].T, tiles_sel, n_q=T, max_back=None, nsub=4,
                        sel_t=sel_p)
    tiles_w = bias_tiles(rel_bias, WIN_A // QB + 1, QB, QB, dstep=QB, window=WIN_A, keys_first=True)
    ow_p = flash_prompt(q, rows_bf_p, 4, rows_bf_p[:, 5 * nkv:6 * nkv].T, tiles_w, n_q=T, max_back=WIN_A // QB)

    pool3 = pool.reshape(n_pool, page * N_SLAB, HEAD_DIM)
    pool_p = pool_compress(page_table, pool3, cmp_pe, cmp_w1)
    n_cmp_s = (past + DEC_T - CMP_LEN) // CMP_STRIDE + 1
    tile_c = bias_tiles(rel_bias, 1, DEC_T, past // CMP_STRIDE, delta0=past, cs=CMP_STRIDE, coff=CMP_LEN - 1,
                        ncols_valid=n_cmp_s)
    oc_s, pen_s = cmp_sample(pool_p, q_s, cmp_w2, tile_c, page=page)
    tiles_past = bias_tiles(rel_bias, 2, DEC_T, page, delta0=2 * page, dstep=-page)
    tile_new = bias_tiles(rel_bias, 1, DEC_T, page)
    os_s = sel_sample(page_table, pool3, q_s, pen_s, rows3, tiles_past, tile_new, g0=n_pg)
    tile_buf = bias_tiles(rel_bias, 1, DEC_T, wb, delta0=wb, window=WIN_A)
    tile_wnew = bias_tiles(rel_bias, 1, DEC_T, LANE, window=WIN_A)
    ow_s, new_win_s = win_sample(q_s, win_state.reshape(B, wb * 2 * KV, HEAD_DIM), rows3, 2, tile_buf, tile_wnew,
                                 g0=n_pg)

    def both(p, s):
        return jnp.concatenate([p, s.reshape(B * DEC_T, nq)], axis=0)

    o = combine(gates, both(oc_p, oc_s), both(os_p, os_s), both(ow_p, ow_s))
    return matmul(o, w_out), rows, new_win_s


def swa_mixer(xn, T, kvsh, w_q, sinks, w_out, rel_bias, state):
    N = xn.shape[0]
    B, wb = state.shape[0], state.shape[1]
    assert wb == WIN_B
    nq, nkv = N_HEADS * HEAD_DIM, KV * HEAD_DIM
    q = matmul(xn, w_q, out_dtype=CDT)
    tiles = bias_tiles(rel_bias, WIN_B // QB + 1, QB, QB, dstep=QB, window=WIN_B, keys_first=True)
    kv_p = kvsh[:T].astype(CDT)
    o_p = flash_prompt(q, kv_p, 0, kv_p[:, nkv:].T, tiles, n_q=T, max_back=WIN_B // QB, sinks=sinks)
    tile_buf = bias_tiles(rel_bias, 1, DEC_T, wb, delta0=wb, window=WIN_B)
    tile_new = bias_tiles(rel_bias, 1, DEC_T, LANE, window=WIN_B)
    o_s, new_state = win_sample(q[T:].astype(F32).reshape(B, DEC_T, nq), state.reshape(B, wb * 2 * KV, HEAD_DIM),
                                kvsh.reshape(N // DEC_T, DEC_T, 2 * nkv), 0, tile_buf, tile_new, g0=T // DEC_T,
                                sinks=sinks)
    o = jnp.concatenate([o_p, o_s.reshape(B * DEC_T, nq)], axis=0).astype(CDT)
    return matmul(o, w_out), new_state


def kernel(x_prompt, x_sample, cache_nsa_kv, state_nsa_win, state_swa_kv, page_table, c_prompt, c_sample,
           ada_w, ada_b, norm_mix, norm_ffn, nsa_w_in, nsa_w_out, nsa_cmp_pe, nsa_cmp_w1, nsa_cmp_w2,
           kv_ada_w, kv_ada_b, kv_norm, kv_w, swa_w_q, swa_sinks, swa_w_out, rel_bias,
           peer_wq, peer_subkeys, peer_u, peer_v, final_norm):
    assert x_prompt.shape[0] == 1 and x_sample.shape[1] == DEC_T
    T, D = x_prompt.shape[1], x_prompt.shape[2]
    B = x_sample.shape[0]
    assert T % QB == 0 and D == N_HEADS * HEAD_DIM
    n_pg = T // DEC_T
    NG = n_pg + B
    N = NG * DEC_T
    nq = N_HEADS * HEAD_DIM
    nkv = KV * HEAD_DIM
    wb_a, wb_b = state_nsa_win.shape[2], state_swa_kv.shape[1]
    assert state_nsa_win.shape[0] == 1 and cache_nsa_kv.shape[0] == 1 and ada_w.shape[0] == 2

    h = jnp.concatenate([x_prompt.reshape(T, D), x_sample.reshape(B * DEC_T, D)], axis=0).reshape(NG, DEC_T, D)
    c = jnp.concatenate([c_prompt, c_sample], axis=0)
    mods = [_group_rows(matmul(c, ada_w[i], ada_b[i], act_in="silu"), n_pg) for i in range(2)]
    mod_kv = _group_rows(matmul(c, kv_ada_w, kv_ada_b, act_in="silu"), n_pg)

    (xn,) = resmod(h, None, [(norm_mix[0], (mods[0], 0, 1))], [CDT], emit_h=False)
    out, rows, new_win_s = nsa_mixer(xn, T, nsa_w_in[0], nsa_w_out[0], nsa_cmp_pe[0], nsa_cmp_w1[0], nsa_cmp_w2[0],
                                     rel_bias, cache_nsa_kv[0], page_table, state_nsa_win[0])
    h, xn = resmod(h, (out.reshape(NG, DEC_T, D), (mods[0], 2)), [(norm_ffn[0], (mods[0], 3, 4))], [CDT], emit_h=True)
    ffn = peer(xn, peer_wq[0], peer_subkeys[0], peer_u, peer_v, 0)

    h, xn, xkv = resmod(h, (ffn.reshape(NG, DEC_T, D), (mods[0], 5)),
                        [(norm_mix[1], (mods[1], 0, 1)), (kv_norm, (mod_kv, 0, 1))], [CDT, CDT], emit_h=True)
    kvsh = matmul(xkv, kv_w)
    out, new_swa_s = swa_mixer(xn, T, kvsh, swa_w_q[0], swa_sinks[0], swa_w_out[0], rel_bias, state_swa_kv)
    h, xn = resmod(h, (out.reshape(NG, DEC_T, D), (mods[1], 2)), [(norm_ffn[1], (mods[1], 3, 4))], [CDT], emit_h=True)
    ffn = peer(xn, peer_wq[1], peer_subkeys[1], peer_u, peer_v, 1)
    (y2d,) = resmod(h, (ffn.reshape(NG, DEC_T, D), (mods[1], 5)), [(final_norm, None)], [F32], emit_h=False)

    rows_p, rows_s = rows[:T], rows[T:]
    return (y2d[:T].reshape(1, T, D),
            y2d[T:].reshape(B, DEC_T, D),
            rows_p[:, :4 * nkv].reshape(1, 1, T, 4, KV, HEAD_DIM),
            rows_s[:, :4 * nkv].reshape(1, B, DEC_T, 4, KV, HEAD_DIM),
            rows_p[T - min(WIN_A, T):, 4 * nkv:].reshape(1, 1, min(WIN_A, T), 2, KV, HEAD_DIM),
            new_win_s.reshape(1, B, wb_a, 2, KV, HEAD_DIM),
            kvsh[T - min(WIN_B, T):T].reshape(1, min(WIN_B, T), 2, KV, HEAD_DIM),
            new_swa_s.reshape(B, wb_b, 2, KV, HEAD_DIM))
```

```python
import functools
import math

import jax
import jax.numpy as jnp
from jax import lax
from jax.experimental import pallas as pl
from jax.experimental.pallas import tpu as pltpu

F32 = jnp.float32
I32 = jnp.int32
CDT = jnp.bfloat16

N_HEADS = 16
HEAD_DIM = 128
KV = 2
G = N_HEADS // KV
SCALE = HEAD_DIM ** -0.5
CMP_LEN = 32
CMP_STRIDE = 16
SEL_BLOCK = 64
SEL_TOPK = 16
N_LOCAL_FORCED = 2
WIN_A = 512
WIN_B = 128
N_BRANCH = 3
FORCED_SCORE = 1e9
NUM_BUCKETS = 32
MAX_EXACT = 16
MAX_DISTANCE = 128
PEER_HEADS = 8
N_KEYS = 128
PEER_TOPK = 16
EPS = 1e-6
SEL_SHIFT = SEL_BLOCK.bit_length() - 1
TOPK_SHIFT = PEER_TOPK.bit_length() - 1
QB = 128
DEC_T = 8
LANE = 128
NEG = -jnp.inf

VMEM_LIMIT = 56 * 1024 * 1024


def _cp(sem=None, vmem=VMEM_LIMIT):
    return pltpu.CompilerParams(dimension_semantics=sem, vmem_limit_bytes=vmem)


def _nt(a, b):
    return lax.dot_general(a, b, (((1,), (1,)), ((), ())), preferred_element_type=F32)


def _dot(a, b):
    return jnp.dot(a, b, preferred_element_type=F32)


def _iota(shape, axis):
    return lax.broadcasted_iota(I32, shape, axis)


def _mm_kernel(x_ref, w_ref, b_ref, o_ref, *, act_in, act_out):
    x = x_ref[...]
    if act_in == "silu":
        x = x.astype(F32)
        x = x * jax.nn.sigmoid(x)
    acc = _dot(x.astype(CDT), w_ref[...].astype(CDT)) + b_ref[...]
    if act_out == "sigmoid":
        acc = jax.nn.sigmoid(acc)
    o_ref[...] = acc.astype(o_ref.dtype)


def matmul(x, w, b=None, *, act_in=None, act_out=None, out_dtype=F32, tm=512, tn=512):
    M, K = x.shape
    N = w.shape[1]
    tm = tm if M % tm == 0 else M
    tn = tn if N % tn == 0 else N
    if b is None:
        b = jnp.zeros((N,), F32)
    b = b.reshape(1, N).astype(F32)
    return pl.pallas_call(
        functools.partial(_mm_kernel, act_in=act_in, act_out=act_out),
        out_shape=jax.ShapeDtypeStruct((M, N), out_dtype),
        grid=(N // tn, M // tm),
        in_specs=[pl.BlockSpec((tm, K), lambda j, i: (i, 0)),
                  pl.BlockSpec((K, tn), lambda j, i: (0, j)),
                  pl.BlockSpec((1, tn), lambda j, i: (0, j))],
        out_specs=pl.BlockSpec((tm, tn), lambda j, i: (i, j)),
        compiler_params=_cp(("parallel", "parallel")),
        name="matmul",
    )(x, w, b)


def _resmod_kernel(*refs, has_res, mod_flags, emit_h):
    it = iter(refs)
    h = next(it)[...]
    if has_res:
        o_ref, g_ref = next(it), next(it)
        h = h + g_ref[...] * o_ref[...]
    mods = []
    for has_mod in mod_flags:
        gain = next(it)
        mods.append((gain, next(it), next(it)) if has_mod else (gain, None, None))
    if emit_h:
        next(it)[...] = h
    y = h * lax.rsqrt(jnp.mean(h * h, axis=-1, keepdims=True) + EPS)
    for gain, sh, sc in mods:
        z = y * gain[...]
        if sh is not None:
            z = z * (1.0 + sc[...]) + sh[...]
        out = next(it)
        out[...] = z.reshape(out.shape).astype(out.dtype)


def resmod(h, res, mods, out_dtypes, *, emit_h, gb=16):
    NG, R, D = h.shape
    gb = gb if NG % gb == 0 else NG
    tok = pl.BlockSpec((gb, R, D), lambda i: (i, 0, 0))
    tok2 = pl.BlockSpec((gb * R, D), lambda i: (i, 0))

    def chunk(c):
        return pl.BlockSpec((gb, 1, D), lambda i, c=c: (i, 0, c))

    args, specs = [h], [tok]
    if res is not None:
        out, (garr, gc) = res
        args += [out, garr]
        specs += [tok, chunk(gc)]
    flags = []
    for gain, m in mods:
        args.append(gain.reshape(1, 1, D))
        specs.append(pl.BlockSpec((1, 1, D), lambda i: (0, 0, 0)))
        flags.append(m is not None)
        if m is not None:
            marr, shc, scc = m
            args += [marr, marr]
            specs += [chunk(shc), chunk(scc)]
    out_shape, out_specs = [], []
    if emit_h:
        out_shape.append(jax.ShapeDtypeStruct(h.shape, F32))
        out_specs.append(tok)
    for dt in out_dtypes:
        out_shape.append(jax.ShapeDtypeStruct((NG * R, D), dt))
        out_specs.append(tok2)
    return pl.pallas_call(
        functools.partial(_resmod_kernel, has_res=res is not None, mod_flags=tuple(flags), emit_h=emit_h),
        out_shape=out_shape, grid=(NG // gb,), in_specs=specs, out_specs=out_specs,
        compiler_params=_cp(("parallel",)), name="resmod",
    )(*args)


def _t5_bucket(dist):
    n = jnp.maximum(dist, 0)
    nf = jnp.maximum(n, 1).astype(F32)
    large = MAX_EXACT + (jnp.log(nf / MAX_EXACT) / math.log(MAX_DISTANCE / MAX_EXACT)
                         * (NUM_BUCKETS - MAX_EXACT)).astype(I32)
    large = jnp.minimum(large, NUM_BUCKETS - 1)
    return jnp.where(n < MAX_EXACT, n, large)


def _bias_lookup(bucket, rb_ref, h):
    out = jnp.full(bucket.shape, rb_ref[NUM_BUCKETS - 1, h], F32)
    for b in range(NUM_BUCKETS - 1):
        out = jnp.where(bucket == b, rb_ref[b, h], out)
    return out


def _bias_tile_kernel(rb_ref, o_ref, *, delta0, dstep, cs, coff, window, ncols_valid, keys_first):
    ti = pl.program_id(0)
    h = pl.program_id(1)
    shape = o_ref.shape[-2:]
    i = _iota(shape, 1 if keys_first else 0)
    j = _iota(shape, 0 if keys_first else 1)
    dist = delta0 + ti * dstep + i - (j * cs + coff)
    ok = dist >= 0
    if window is not None:
        ok = ok & (dist <= window)
    if ncols_valid is not None:
        ok = ok & (j < ncols_valid)
    o_ref[0, 0] = jnp.where(ok, _bias_lookup(_t5_bucket(dist), rb_ref, h), NEG)


def bias_tiles(rel_bias, n_tiles, rows, cols, *, delta0=0, dstep=0, cs=1, coff=0, window=None, ncols_valid=None,
               keys_first=False):
    return pl.pallas_call(
        functools.partial(_bias_tile_kernel, delta0=delta0, dstep=dstep, cs=cs, coff=coff, window=window,
                          ncols_valid=ncols_valid, keys_first=keys_first),
        out_shape=jax.ShapeDtypeStruct((n_tiles, N_HEADS, rows, cols), F32),
        grid=(n_tiles, N_HEADS),
        in_specs=[pl.BlockSpec(memory_space=pltpu.SMEM)],
        out_specs=pl.BlockSpec((1, 1, rows, cols), lambda t, h: (t, h, 0, 0)),
        compiler_params=_cp(("parallel", "parallel")), name="bias_tiles",
    )(rel_bias)


def _softmax_rows(s):
    m = jnp.max(s, axis=-1, keepdims=True)
    m = jnp.where(m == NEG, 0.0, m)
    e = jnp.exp(s - m)
    z = jnp.sum(e, axis=-1, keepdims=True)
    return e / jnp.where(z > 0, z, 1.0)


def _flash_step(s, v, m, l, acc):
    gg, r, kk = s.shape
    m_new = jnp.maximum(m, jnp.max(s, axis=-1, keepdims=True))
    m_safe = jnp.where(m_new == NEG, 0.0, m_new)
    alpha = jnp.exp(m - m_safe)
    p = jnp.exp(s - m_safe)
    l = alpha * l + jnp.sum(p, axis=-1, keepdims=True)
    pv = _dot(p.reshape(gg * r, kk).astype(CDT), v).reshape(gg, r, HEAD_DIM)
    return m_new, l, alpha * acc + pv


def _flash_finish(m, l, acc, sinks):
    if sinks is not None:
        m_f = jnp.maximum(m, sinks)
        a = jnp.exp(m - m_f)
        l = l * a + jnp.exp(sinks - m_f)
        acc = acc * a
    return acc / jnp.where(l > 0, l, 1.0)


def _head_rows(q, kv):
    return jnp.concatenate([q[:, (kv * G + g) * HEAD_DIM:(kv * G + g + 1) * HEAD_DIM] for g in range(G)],
                           axis=0).astype(CDT)


def _sink_col(sink_ref, kv):
    return jnp.concatenate([jnp.full((1, 1, 1), sink_ref[kv * G + g], F32) for g in range(G)], axis=0)


def _flash_prompt_kernel(*refs, n_tiles, max_back, has_sel, has_sinks, nsub):
    it = iter(refs)
    q_ref, k_ref, vt_ref, t_ref = next(it), next(it), next(it), next(it)
    sel_ref = next(it) if has_sel else None
    sink_ref = next(it) if has_sinks else None
    o_ref, m_ref, l_ref, acc_ref = next(it), next(it), next(it), next(it)
    qi = pl.program_id(0)
    kc = nsub * QB
    c_lo = 0 if max_back is None else jnp.maximum(qi - max_back, 0) // nsub
    c_hi = qi // nsub
    for kv in range(KV):
        cols = slice(kv * HEAD_DIM, (kv + 1) * HEAD_DIM)
        m_ref[...] = jnp.full(m_ref.shape, NEG, F32)
        l_ref[...] = jnp.zeros(l_ref.shape, F32)
        acc_ref[...] = jnp.zeros(acc_ref.shape, F32)
        qs = [q_ref[:, (kv * G + g) * HEAD_DIM:(kv * G + g + 1) * HEAD_DIM] for g in range(G)]

        def body(c, carry, kv=kv, qs=qs, cols=cols):
            start = pl.multiple_of(c * kc, kc)
            ks = k_ref[pl.ds(start, kc), cols]
            vts = vt_ref[cols, pl.ds(start, kc)]
            tidx = []
            for j in range(nsub):
                d = qi - (c * nsub + j)
                dead = d < 0 if max_back is None else (d < 0) | (d > max_back)
                tidx.append(jnp.where(dead, n_tiles, jnp.minimum(d, n_tiles - 1)))
            if has_sel:
                nj = sel_ref.shape[1]
                blk = (kc // SEL_BLOCK) * c + (_iota((kc, nj), 0) >> SEL_SHIFT)
                hit = _dot((_iota((kc, nj), 1) == blk).astype(CDT), sel_ref[kv])
                pen = jnp.where(hit > 0.5, 0.0, NEG)
            for g in range(G):
                h = kv * G + g
                s = _nt(ks, qs[g]) * SCALE + jnp.concatenate([t_ref[tidx[j], h] for j in range(nsub)], axis=0)
                if has_sel:
                    s = s + pen
                m = m_ref[g]
                m_new = jnp.maximum(m, jnp.max(s, axis=0, keepdims=True))
                m_safe = jnp.where(m_new == NEG, 0.0, m_new)
                alpha = jnp.exp(m - m_safe)
                p = jnp.exp(s - m_safe)
                m_ref[g] = m_new
                l_ref[g] = alpha * l_ref[g] + jnp.sum(p, axis=0, keepdims=True)
                acc_ref[g] = alpha * acc_ref[g] + _dot(vts, p.astype(CDT))
            return carry

        lax.fori_loop(c_lo, c_hi + 1, body, 0)
        for g in range(G):
            h = kv * G + g
            m, l, acc = m_ref[g], l_ref[g], acc_ref[g]
            if has_sinks:
                m_f = jnp.maximum(m, sink_ref[h])
                a = jnp.exp(m - m_f)
                l = l * a + jnp.exp(sink_ref[h] - m_f)
                acc = acc * a
            o_ref[:, h * HEAD_DIM:(h + 1) * HEAD_DIM] = (acc / jnp.where(l > 0, l, 1.0)).T


def flash_prompt(q, k_arr, kblk, v_t, tiles_t, *, n_q, max_back, sel_t=None, sinks=None, nsub=2):
    T = k_arr.shape[0]
    n_tiles = tiles_t.shape[0]
    assert T % (nsub * QB) == 0
    tiles_t = jnp.concatenate([tiles_t, jnp.full((1,) + tiles_t.shape[1:], NEG, F32)], axis=0)
    w = KV * HEAD_DIM
    args = [q, k_arr, v_t, tiles_t]
    specs = [pl.BlockSpec((QB, N_HEADS * HEAD_DIM), lambda i: (i, 0)),
             pl.BlockSpec((T, w), lambda i: (0, kblk)),
             pl.BlockSpec((w, T), lambda i: (0, 0)),
             pl.BlockSpec(tiles_t.shape, lambda i: (0, 0, 0, 0))]
    scratch = [pltpu.VMEM((G, 1, QB), F32), pltpu.VMEM((G, 1, QB), F32), pltpu.VMEM((G, HEAD_DIM, QB), F32)]
    if sel_t is not None:
        args.append(sel_t)
        specs.append(pl.BlockSpec((KV, sel_t.shape[1], QB), lambda i: (0, 0, i)))
    if sinks is not None:
        args.append(sinks)
        specs.append(pl.BlockSpec(memory_space=pltpu.SMEM))
    return pl.pallas_call(
        functools.partial(_flash_prompt_kernel, n_tiles=n_tiles, max_back=max_back, has_sel=sel_t is not None,
                          has_sinks=sinks is not None, nsub=nsub),
        out_shape=jax.ShapeDtypeStruct((n_q, N_HEADS * HEAD_DIM), F32),
        grid=(n_q // QB,), in_specs=specs,
        out_specs=pl.BlockSpec((QB, N_HEADS * HEAD_DIM), lambda i: (i, 0)),
        scratch_shapes=scratch,
        compiler_params=_cp(("parallel",)), name="flash_prompt",
    )(*args)


def _compress_rows(x, pe_ref, w1_ref, acc_ref, first):
    @pl.when(first)
    def _():
        acc_ref[...] = jnp.zeros(acc_ref.shape, F32)

    for gi in range(2 * KV):
        comp = gi // KV
        xg = x[:, gi * HEAD_DIM:(gi + 1) * HEAD_DIM]
        for a in range(2):
            xa = (xg + pe_ref[comp, a, 0]).astype(CDT)
            col = slice((gi * 2 + a) * HEAD_DIM, (gi * 2 + a + 1) * HEAD_DIM)
            acc_ref[:, col] += _dot(xa, w1_ref[comp, a, 0].astype(CDT))


def _compress_finish(p, w2_ref, gi):
    c = p.shape[0]
    pa = p[:, (gi * 2) * HEAD_DIM:(gi * 2 + 1) * HEAD_DIM]
    pb = p[:, (gi * 2 + 1) * HEAD_DIM:(gi * 2 + 2) * HEAD_DIM]
    h1 = jax.nn.gelu(pa + pltpu.roll(pb, c - 1, 0))
    return _dot(h1.astype(CDT), w2_ref[gi // KV].astype(CDT))


def _compress_prompt_kernel(x_ref, pe_ref, w1_ref, w2_ref, o_ref, acc_ref):
    b = pl.program_id(0)
    _compress_rows(x_ref[...], pe_ref, w1_ref, acc_ref, b == 0)

    @pl.when(b == CMP_STRIDE - 1)
    def _():
        p = acc_ref[...]
        for gi in range(2 * KV):
            o_ref[gi] = _compress_finish(p, w2_ref, gi)


def _cmp_weights(pe, w1):
    pe5 = pe.reshape(2, 2, CMP_STRIDE, 1, HEAD_DIM)
    w15 = w1.reshape(2, 2, CMP_STRIDE, HEAD_DIM, HEAD_DIM)
    return pe5, w15


def compress_prompt(rows, pe, w1, w2):
    T = rows.shape[0]
    C = T // CMP_STRIDE
    x = rows.reshape(C, CMP_STRIDE * rows.shape[1])
    nblk = rows.shape[1] // (4 * HEAD_DIM)
    pe5, w15 = _cmp_weights(pe, w1)
    return pl.pallas_call(
        _compress_prompt_kernel,
        out_shape=jax.ShapeDtypeStruct((2 * KV, C, HEAD_DIM), F32),
        grid=(CMP_STRIDE,),
        in_specs=[pl.BlockSpec((C, 4 * HEAD_DIM), lambda b: (0, b * nblk)),
                  pl.BlockSpec((2, 2, 1, 1, HEAD_DIM), lambda b: (0, 0, b, 0, 0)),
                  pl.BlockSpec((2, 2, 1, HEAD_DIM, HEAD_DIM), lambda b: (0, 0, b, 0, 0)),
                  pl.BlockSpec((2, HEAD_DIM, HEAD_DIM), lambda b: (0, 0, 0))],
        out_specs=pl.BlockSpec((2 * KV, C, HEAD_DIM), lambda b: (0, 0, 0)),
        scratch_shapes=[pltpu.VMEM((C, 8 * HEAD_DIM), F32)],
        compiler_params=_cp(("arbitrary",)), name="compress_prompt",
    )(x, pe5, w15, w2)


def _overlap(n_idx, j_idx, n_cmp, n_slc):
    return ((n_idx * CMP_STRIDE <= j_idx * SEL_BLOCK + SEL_BLOCK - 1)
            & (n_idx * CMP_STRIDE + CMP_LEN - 1 >= j_idx * SEL_BLOCK)
            & (n_idx < n_cmp) & (j_idx < n_slc))


def _split_hi_lo(x):
    hi = x.astype(CDT)
    return hi, (x - hi.astype(F32)).astype(CDT)


def _sel_scores(imp, j_idx, t, n_slc):
    valid = (j_idx * SEL_BLOCK <= t) & (j_idx < n_slc)
    cur = t >> SEL_SHIFT
    forced = (j_idx == 0) | ((j_idx <= cur) & (j_idx > cur - N_LOCAL_FORCED))
    return jnp.where(valid, jnp.where(forced, FORCED_SCORE, imp), NEG)


def _topk_mask(score, idx, axis, k):
    n = score.shape[axis]
    sel = jnp.zeros(score.shape, F32)
    for _ in range(k):
        m = jnp.max(score, axis=axis, keepdims=True)
        first = jnp.min(jnp.where(score == m, idx, n), axis=axis, keepdims=True)
        hit = idx == first
        sel = jnp.where(hit & (m > NEG), 1.0, sel)
        score = jnp.where(hit, NEG, score)
    return sel


def _cmp_prompt_kernel(q_ref, kc_ref, rb_ref, oc_ref, sel_ref, *, n_cmp, n_slc, k_top):
    qi = pl.program_id(0)
    t0 = qi * QB
    C = kc_ref.shape[1]
    J = sel_ref.shape[1]
    nw = min(LANE, C)
    per_q = QB // CMP_STRIDE
    assert CMP_STRIDE * nw - (QB + CMP_LEN - 1) + CMP_STRIDE >= MAX_DISTANCE
    nb = pl.multiple_of(jnp.clip(per_q * (qi + 1) - nw, 0, C - nw), 8)
    far_ok = _iota((QB, C), 1) < nb
    n_near = nb + _iota((QB, nw), 1)
    dist = t0 + _iota((QB, nw), 0) - (n_near * CMP_STRIDE + CMP_LEN - 1)
    near_ok = (dist >= 0) & (n_near < n_cmp)
    bucket = _t5_bucket(dist)
    ov_far = _overlap(_iota((J, C), 1), _iota((J, C), 0), n_cmp, n_slc).astype(CDT)
    ov_near = _overlap(nb + _iota((J, nw), 1), _iota((J, nw), 0), n_cmp, n_slc).astype(CDT)
    j_t = _iota((J, QB), 0)
    t_t = t0 + _iota((J, QB), 1)
    q = q_ref[...]
    for kv in range(KV):
        q2 = _head_rows(q, kv)
        far_bias = jnp.concatenate([jnp.full((1, 1, 1), rb_ref[NUM_BUCKETS - 1, kv * G + g], F32) for g in range(G)],
                                   axis=0)
        s_far = _nt(q2, kc_ref[kv].astype(CDT)).reshape(G, QB, C) * SCALE + far_bias
        s_far = jnp.where(far_ok[None], s_far, NEG)
        bias = jnp.stack([_bias_lookup(bucket, rb_ref, kv * G + g) for g in range(G)], axis=0)
        s_near = _nt(q2, kc_ref[kv, pl.ds(nb, nw), :].astype(CDT)).reshape(G, QB, nw) * SCALE + bias
        s_near = jnp.where(near_ok[None], s_near, NEG)
        m = jnp.maximum(jnp.max(s_far, axis=-1, keepdims=True), jnp.max(s_near, axis=-1, keepdims=True))
        m = jnp.where(m == NEG, 0.0, m)
        e_far = jnp.exp(s_far - m)
        e_near = jnp.exp(s_near - m)
        z = jnp.sum(e_far, axis=-1, keepdims=True) + jnp.sum(e_near, axis=-1, keepdims=True)
        z = jnp.where(z > 0, z, 1.0)
        p_far = e_far / z
        p_near = e_near / z
        o = (_dot(p_far.reshape(G * QB, C).astype(CDT), kc_ref[KV + kv].astype(CDT))
             + _dot(p_near.reshape(G * QB, nw).astype(CDT), kc_ref[KV + kv, pl.ds(nb, nw), :].astype(CDT)))
        o = o.reshape(G, QB, HEAD_DIM)
        for g in range(G):
            oc_ref[:, (kv * G + g) * HEAD_DIM:(kv * G + g + 1) * HEAD_DIM] = o[g]
        hi_f, lo_f = _split_hi_lo(jnp.sum(p_far, axis=0))
        hi_n, lo_n = _split_hi_lo(jnp.sum(p_near, axis=0))
        imp_t = _nt(ov_far, hi_f) + _nt(ov_far, lo_f) + _nt(ov_near, hi_n) + _nt(ov_near, lo_n)
        sel_t = _topk_mask(_sel_scores(imp_t, j_t, t_t, n_slc), j_t, 0, k_top)
        sel_ref[kv] = sel_t.astype(sel_ref.dtype)


def cmp_prompt(q, kcvc, rel_bias, *, n_q):
    C = kcvc.shape[1]
    n_cmp = (n_q - CMP_LEN) // CMP_STRIDE + 1
    n_slc = -(-n_q // SEL_BLOCK)
    return pl.pallas_call(
        functools.partial(_cmp_prompt_kernel, n_cmp=n_cmp, n_slc=n_slc, k_top=min(SEL_TOPK, n_slc)),
        out_shape=[jax.ShapeDtypeStruct((n_q, N_HEADS * HEAD_DIM), F32),
                   jax.ShapeDtypeStruct((KV, n_slc, n_q), CDT)],
        grid=(n_q // QB,),
        in_specs=[pl.BlockSpec((QB, N_HEADS * HEAD_DIM), lambda i: (i, 0)),
                  pl.BlockSpec(kcvc.shape, lambda i: (0, 0, 0)),
                  pl.BlockSpec(memory_space=pltpu.SMEM)],
        out_specs=[pl.BlockSpec((QB, N_HEADS * HEAD_DIM), lambda i: (i, 0)),
                   pl.BlockSpec((KV, n_slc, QB), lambda i: (0, 0, i))],
        compiler_params=_cp(("parallel",)), name="cmp_prompt",
    )(q, kcvc, rel_bias)


def _combine_kernel(g_ref, c_ref, s_ref, w_ref, o_ref):
    gates = g_ref[...]
    for h in range(N_HEADS):
        col = slice(h * HEAD_DIM, (h + 1) * HEAD_DIM)
        o = (gates[:, N_BRANCH * h:N_BRANCH * h + 1] * c_ref[:, col]
             + gates[:, N_BRANCH * h + 1:N_BRANCH * h + 2] * s_ref[:, col]
             + gates[:, N_BRANCH * h + 2:N_BRANCH * h + 3] * w_ref[:, col])
        o_ref[:, col] = o.astype(o_ref.dtype)


def combine(gates, oc, os_, ow, tm=256):
    N, D = oc.shape
    assert N % tm == 0
    tok = pl.BlockSpec((tm, D), lambda i: (i, 0))
    return pl.pallas_call(
        _combine_kernel, out_shape=jax.ShapeDtypeStruct((N, D), CDT), grid=(N // tm,),
        in_specs=[pl.BlockSpec((tm, gates.shape[1]), lambda i: (i, 0)), tok, tok, tok], out_specs=tok,
        compiler_params=_cp(("parallel",)), name="nsa_combine",
    )(gates, oc, os_, ow)


N_SLAB = 4 * KV


def _page_slab(p_ref, slab, row0, n, step):
    return p_ref[0, pl.ds(row0 * N_SLAB + slab, n, stride=step * N_SLAB), :]


def _pool_cmp_kernel(pt_ref, *refs, n_ops, cpp):
    p_refs = refs[:n_ops]
    pe_ref, w1_ref, o_ref = refs[n_ops:]
    nc = n_ops * cpp
    for comp in range(2):
        x = jnp.concatenate(
            [jnp.concatenate([jnp.concatenate([_page_slab(p, comp * KV + kv, b, cpp, CMP_STRIDE) for p in p_refs],
                                              axis=0) for b in range(CMP_STRIDE)], axis=1) for kv in range(KV)],
            axis=0)
        for a in range(2):
            d = _dot((x + pe_ref[comp, a]).astype(CDT), w1_ref[comp, a])
            for kv in range(KV):
                col = ((comp * KV + kv) * 2 + a) * HEAD_DIM
                o_ref[:, col:col + HEAD_DIM] = d[kv * nc:(kv + 1) * nc]


def pool_compress(page_table, pool3, pe, w1, n_ops=16):
    B, n_pages = page_table.shape
    page = pool3.shape[1] // N_SLAB
    cpp = page // CMP_STRIDE
    n_ops = n_ops if n_pages % n_ops == 0 else n_pages
    ng = n_pages // n_ops
    half = CMP_STRIDE * HEAD_DIM
    pe4 = pe.reshape(2, 2, 1, half)
    w14 = w1.reshape(2, 2, half, HEAD_DIM).astype(CDT)
    p_specs = [pl.BlockSpec((1, page * N_SLAB, HEAD_DIM), lambda b, g, pt, r=r: (pt[b, g * n_ops + r], 0, 0))
               for r in range(n_ops)]
    grid_spec = pltpu.PrefetchScalarGridSpec(
        num_scalar_prefetch=1, grid=(B, ng),
        in_specs=p_specs + [pl.BlockSpec(pe4.shape, lambda b, g, pt: (0, 0, 0, 0)),
                            pl.BlockSpec(w14.shape, lambda b, g, pt: (0, 0, 0, 0))],
        out_specs=pl.BlockSpec((n_ops * cpp, 8 * HEAD_DIM), lambda b, g, pt: (b * ng + g, 0)))
    return pl.pallas_call(
        functools.partial(_pool_cmp_kernel, n_ops=n_ops, cpp=cpp),
        out_shape=jax.ShapeDtypeStruct((B * n_pages * cpp, 8 * HEAD_DIM), F32),
        grid_spec=grid_spec, compiler_params=_cp(("parallel", "arbitrary")), name="pool_compress",
    )(page_table, *([pool3] * n_ops), pe4, w14)


def _topk_rank_mask(score, k, n_valid):
    nr, nj = score.shape
    jr = -(-n_valid // 8) * 8
    st = jnp.concatenate([score, jnp.full((LANE - nr, nj), NEG, F32)], axis=0).T
    idx_c = _iota((jr, nj), 0)
    idx_r = _iota((jr, nj), 1)
    rows = []
    for r in range(nr):
        col = st[:jr, r:r + 1]
        row = score[r:r + 1, :]
        beats = (col > row) | ((col == row) & (idx_c < idx_r))
        rank = jnp.sum(jnp.where(beats, 1.0, 0.0), axis=0, keepdims=True)
        rows.append(jnp.where((rank < k) & (row > NEG), 1.0, 0.0))
    return jnp.concatenate(rows, axis=0)


def _cmp_sample_kernel(p_ref, q_ref, w2_ref, t_ref, e_ref, oc_ref, pen_ref, *, n_cmp, n_slc, k_top, q_pos0):
    p = p_ref[...]
    C = p.shape[0]
    J = e_ref.shape[0]
    q = q_ref[0]
    ov = _overlap(_iota((C, J), 0), _iota((C, J), 1), n_cmp, n_slc).astype(CDT)
    j_idx = _iota((DEC_T, J), 1)
    t = q_pos0 + _iota((DEC_T, J), 0)
    scores = []
    for kv in range(KV):
        kc = _compress_finish(p, w2_ref, kv).astype(CDT)
        vc = _compress_finish(p, w2_ref, KV + kv).astype(CDT)
        q2 = _head_rows(q, kv)
        s = _nt(q2, kc).reshape(G, DEC_T, C) * SCALE + t_ref[0, kv * G:(kv + 1) * G]
        pr = _softmax_rows(s)
        o = _dot(pr.reshape(G * DEC_T, C).astype(CDT), vc)
        for g in range(G):
            oc_ref[0, :, (kv * G + g) * HEAD_DIM:(kv * G + g + 1) * HEAD_DIM] = o[g * DEC_T:(g + 1) * DEC_T]
        hi, lo = _split_hi_lo(jnp.sum(pr, axis=0))
        imp = _dot(hi, ov) + _dot(lo, ov)
        scores.append(_sel_scores(imp, j_idx, t, n_slc))
    sel = _topk_rank_mask(jnp.concatenate(scores, axis=0), k_top, n_slc)
    pen_ref[0] = jnp.where(_dot(sel.astype(CDT), e_ref[...]) > 0.5, 0.0, NEG)


def cmp_sample(pool_p, q3, w2, tile_c, *, page):
    B = q3.shape[0]
    C = pool_p.shape[0] // B
    past = C * CMP_STRIDE
    L = past + DEC_T
    n_cmp = (L - CMP_LEN) // CMP_STRIDE + 1
    n_slc = -(-L // SEL_BLOCK)
    jp = -(-n_slc // LANE) * LANE
    D = N_HEADS * HEAD_DIM
    nk = past + page
    e_mat = (jnp.arange(jp, dtype=I32)[:, None] == jnp.arange(nk, dtype=I32)[None, :] // SEL_BLOCK).astype(CDT)
    return pl.pallas_call(
        functools.partial(_cmp_sample_kernel, n_cmp=n_cmp, n_slc=n_slc, k_top=min(SEL_TOPK, n_slc), q_pos0=past),
        out_shape=[jax.ShapeDtypeStruct((B, DEC_T, D), F32), jax.ShapeDtypeStruct((B, KV * DEC_T, nk), F32)],
        grid=(B,),
        in_specs=[pl.BlockSpec((C, 8 * HEAD_DIM), lambda b: (b, 0)),
                  pl.BlockSpec((1, DEC_T, D), lambda b: (b, 0, 0)),
                  pl.BlockSpec(w2.shape, lambda b: (0, 0, 0)),
                  pl.BlockSpec(tile_c.shape, lambda b: (0, 0, 0, 0)),
                  pl.BlockSpec((jp, nk), lambda b: (0, 0))],
        out_specs=[pl.BlockSpec((1, DEC_T, D), lambda b: (b, 0, 0)),
                   pl.BlockSpec((1, KV * DEC_T, nk), lambda b: (b, 0, 0))],
        compiler_params=_cp(("parallel",)), name="cmp_sample",
    )(pool_p, q3, w2, tile_c, e_mat)


def _sel_sample_kernel(pt_ref, *refs, n_ops, n_pages):
    p_refs = refs[:n_ops]
    q_ref, pen_ref, new_ref, tp_ref, tn_ref, o_ref, m_ref, l_ref, acc_ref = refs[n_ops:]
    gstep = pl.program_id(1)
    page = p_refs[0].shape[1] // N_SLAB
    nk = n_ops * page

    @pl.when(gstep == 0)
    def _():
        m_ref[...] = jnp.full(m_ref.shape, NEG, F32)
        l_ref[...] = jnp.zeros(l_ref.shape, F32)
        acc_ref[...] = jnp.zeros(acc_ref.shape, F32)

    q = q_ref[0]

    def update(kv, k, v, bias, pen):
        s = _nt(_head_rows(q, kv), k).reshape(G, DEC_T, k.shape[0]) * SCALE + bias + pen[None]
        m, l, acc = _flash_step(s, v, m_ref[kv], l_ref[kv], acc_ref[kv])
        m_ref[kv] = m
        l_ref[kv] = l
        acc_ref[kv] = acc

    for kv in range(KV):
        k = jnp.concatenate([_page_slab(p, 2 * KV + kv, 0, page, 1) for p in p_refs], axis=0).astype(CDT)
        v = jnp.concatenate([_page_slab(p, 3 * KV + kv, 0, page, 1) for p in p_refs], axis=0).astype(CDT)
        bias = jnp.concatenate(
            [tp_ref[jnp.where(gstep * n_ops + r == n_pages - 1, 1, 0), kv * G:(kv + 1) * G] for r in range(n_ops)],
            axis=-1)
        pen = pen_ref[0, kv * DEC_T:(kv + 1) * DEC_T, pl.ds(pl.multiple_of(gstep * nk, nk), nk)]
        update(kv, k, v, bias, pen)

    @pl.when(gstep == pl.num_programs(1) - 1)
    def _():
        new = new_ref[0]
        pad = jnp.zeros((page - DEC_T, HEAD_DIM), F32)
        past = n_pages * page
        for kv in range(KV):
            k = jnp.concatenate([new[:, kv * HEAD_DIM:(kv + 1) * HEAD_DIM], pad], axis=0).astype(CDT)
            v = jnp.concatenate([new[:, (KV + kv) * HEAD_DIM:(KV + kv + 1) * HEAD_DIM], pad], axis=0).astype(CDT)
            update(kv, k, v, tn_ref[0, kv * G:(kv + 1) * G], pen_ref[0, kv * DEC_T:(kv + 1) * DEC_T, past:past + page])
            o = _flash_finish(m_ref[kv], l_ref[kv], acc_ref[kv], None)
            for g in range(G):
                o_ref[0, :, (kv * G + g) * HEAD_DIM:(kv * G + g + 1) * HEAD_DIM] = o[g]


def sel_sample(page_table, pool3, q3, pen, rows3, tiles_past, tile_new, *, g0, n_ops=16):
    B, n_pages = page_table.shape
    page = pool3.shape[1] // N_SLAB
    n_ops = n_ops if n_pages % n_ops == 0 else n_pages
    D = N_HEADS * HEAD_DIM
    w = 2 * KV * HEAD_DIM
    p_specs = [pl.BlockSpec((1, page * N_SLAB, HEAD_DIM), lambda b, g, pt, r=r: (pt[b, g * n_ops + r], 0, 0))
               for r in range(n_ops)]
    grid_spec = pltpu.PrefetchScalarGridSpec(
        num_scalar_prefetch=1, grid=(B, n_pages // n_ops),
        in_specs=p_specs + [pl.BlockSpec((1, DEC_T, D), lambda b, g, pt: (b, 0, 0)),
                            pl.BlockSpec((1,) + pen.shape[1:], lambda b, g, pt: (b, 0, 0)),
                            pl.BlockSpec((1, DEC_T, w), lambda b, g, pt: (g0 + b, 0, 1)),
                            pl.BlockSpec(tiles_past.shape, lambda b, g, pt: (0, 0, 0, 0)),
                            pl.BlockSpec(tile_new.shape, lambda b, g, pt: (0, 0, 0, 0))],
        out_specs=pl.BlockSpec((1, DEC_T, D), lambda b, g, pt: (b, 0, 0)),
        scratch_shapes=[pltpu.VMEM((KV, G, DEC_T, 1), F32), pltpu.VMEM((KV, G, DEC_T, 1), F32),
                        pltpu.VMEM((KV, G, DEC_T, HEAD_DIM), F32)])
    return pl.pallas_call(
        functools.partial(_sel_sample_kernel, n_ops=n_ops, n_pages=n_pages),
        out_shape=jax.ShapeDtypeStruct((B, DEC_T, D), F32),
        grid_spec=grid_spec, compiler_params=_cp(("parallel", "arbitrary")), name="sel_sample",
    )(page_table, *([pool3] * n_ops), q3, pen, rows3, tiles_past, tile_new)


def _win_sample_kernel(*refs, has_sinks):
    it = iter(refs)
    q_ref, buf_ref, new_ref, tb_ref, tn_ref = next(it), next(it), next(it), next(it), next(it)
    sink_ref = next(it) if has_sinks else None
    o_ref, st_ref = next(it), next(it)
    q = q_ref[0]
    new = new_ref[0]
    ns = 2 * KV
    wb = buf_ref.shape[1] // ns
    npad = tn_ref.shape[-1]
    pad = jnp.zeros((npad - DEC_T, HEAD_DIM), F32)
    for kv in range(KV):
        kc = slice(kv * HEAD_DIM, (kv + 1) * HEAD_DIM)
        vc = slice((KV + kv) * HEAD_DIM, (KV + kv + 1) * HEAD_DIM)
        k = jnp.concatenate([buf_ref[0, pl.ds(kv, wb, stride=ns), :], new[:, kc], pad], axis=0).astype(CDT)
        v = jnp.concatenate([buf_ref[0, pl.ds(KV + kv, wb, stride=ns), :], new[:, vc], pad], axis=0).astype(CDT)
        bias = jnp.concatenate([tb_ref[0, kv * G:(kv + 1) * G], tn_ref[0, kv * G:(kv + 1) * G]], axis=-1)
        s = _nt(_head_rows(q, kv), k).reshape(G, DEC_T, wb + npad) * SCALE + bias
        init = (jnp.full((G, DEC_T, 1), NEG, F32), jnp.zeros((G, DEC_T, 1), F32), jnp.zeros((G, DEC_T, HEAD_DIM), F32))
        m, l, acc = _flash_step(s, v, *init)
        o = _flash_finish(m, l, acc, _sink_col(sink_ref, kv) if has_sinks else None)
        for g in range(G):
            o_ref[0, :, (kv * G + g) * HEAD_DIM:(kv * G + g + 1) * HEAD_DIM] = o[g]
    st_ref[0, :(wb - DEC_T) * ns, :] = buf_ref[0, DEC_T * ns:, :]
    for s in range(ns):
        st_ref[0, pl.ds((wb - DEC_T) * ns + s, DEC_T, stride=ns), :] = new[:, s * HEAD_DIM:(s + 1) * HEAD_DIM]


def win_sample(q3, buf, new3, new_blk, tile_buf, tile_new, *, g0, sinks=None):
    B, nrow = buf.shape[0], buf.shape[1]
    w = 2 * KV * HEAD_DIM
    D = N_HEADS * HEAD_DIM
    args = [q3, buf, new3, tile_buf, tile_new]
    specs = [pl.BlockSpec((1, DEC_T, D), lambda b: (b, 0, 0)),
             pl.BlockSpec((1, nrow, HEAD_DIM), lambda b: (b, 0, 0)),
             pl.BlockSpec((1, DEC_T, w), lambda b: (g0 + b, 0, new_blk)),
             pl.BlockSpec(tile_buf.shape, lambda b: (0, 0, 0, 0)),
             pl.BlockSpec(tile_new.shape, lambda b: (0, 0, 0, 0))]
    if sinks is not None:
        args.append(sinks)
        specs.append(pl.BlockSpec(memory_space=pltpu.SMEM))
    return pl.pallas_call(
        functools.partial(_win_sample_kernel, has_sinks=sinks is not None),
        out_shape=[jax.ShapeDtypeStruct((B, DEC_T, D), F32), jax.ShapeDtypeStruct(buf.shape, F32)],
        grid=(B,), in_specs=specs,
        out_specs=[pl.BlockSpec((1, DEC_T, D), lambda b: (b, 0, 0)),
                   pl.BlockSpec((1, nrow, HEAD_DIM), lambda b: (b, 0, 0))],
        compiler_params=_cp(("parallel",)), name="win_sample",
    )(*args)


def _topk_rows(s, k):
    r = s.shape[0]
    idx = _iota(s.shape, 0)
    vals, ids = [], []
    for _ in range(k):
        m = jnp.max(s, axis=0, keepdims=True)
        first = jnp.min(jnp.where(s == m, idx, r), axis=0, keepdims=True)
        vals.append(m)
        ids.append(first)
        s = jnp.where(idx == first, NEG, s)
    return jnp.concatenate(vals, axis=0), jnp.concatenate(ids, axis=0)


def _peer_topk_kernel(q_ref, sk_ref, ik_ref, jk_ref, g_ref):
    half = sk_ref.shape[-1]
    sv, si = [], []
    for c in range(2):
        s_t = _nt(sk_ref[0, c].astype(CDT), q_ref[:, c * half:(c + 1) * half].astype(CDT))
        v, i = _topk_rows(s_t, PEER_TOPK)
        sv.append(v)
        si.append(i)
    blocks, starts, offs = [], [], []
    r0 = 0
    for a in range(PEER_TOPK):
        nb = PEER_TOPK // (a + 1)
        blocks.append(sv[0][a:a + 1] + sv[1][:nb])
        starts.append(r0)
        offs.append(a * PEER_TOPK - r0)
        r0 += nb
    blocks.append(jnp.full((-r0 % 8, sv[0].shape[1]), NEG, F32))
    top, cr = _topk_rows(jnp.concatenate(blocks, axis=0), PEER_TOPK)
    ci = cr + offs[0]
    for a in range(1, PEER_TOPK):
        ci = jnp.where(cr >= starts[a], cr + offs[a], ci)
    a_idx = ci >> TOPK_SHIFT
    b_idx = ci & (PEER_TOPK - 1)
    isel = jnp.zeros(ci.shape, I32)
    jsel = jnp.zeros(ci.shape, I32)
    for a in range(PEER_TOPK):
        isel = jnp.where(a_idx == a, si[0][a:a + 1], isel)
        jsel = jnp.where(b_idx == a, si[1][a:a + 1], jsel)
    e = jnp.exp(top - top[0:1])
    ik_ref[...] = isel
    jk_ref[...] = jsel
    g_ref[...] = e / jnp.sum(e, axis=0, keepdims=True)


def peer_topk(q, subkeys, tm=256):
    N = q.shape[0]
    assert N % tm == 0
    half = subkeys.shape[-1]
    outs = pl.BlockSpec((PEER_TOPK, tm), lambda i, h: (h, i))
    shp = (PEER_HEADS * PEER_TOPK, N)
    return pl.pallas_call(
        _peer_topk_kernel,
        out_shape=[jax.ShapeDtypeStruct(shp, I32), jax.ShapeDtypeStruct(shp, I32), jax.ShapeDtypeStruct(shp, F32)],
        grid=(N // tm, PEER_HEADS),
        in_specs=[pl.BlockSpec((tm, 2 * half), lambda i, h: (i, h)),
                  pl.BlockSpec((1, 2, N_KEYS, half), lambda i, h: (h, 0, 0, 0))],
        out_specs=[outs, outs, outs],
        compiler_params=_cp(("parallel", "parallel")), name="peer_topk",
    )(q, subkeys)


def _peer_w_kernel(ik_ref, jk_ref, g_ref, w_ref, iks, jks, gs):
    iks[...] = ik_ref[...].T
    jks[...] = jk_ref[...].T
    gs[...] = g_ref[...].T
    hk = iks.shape[1]
    key = _iota((N_KEYS, hk), 0)

    def body(tg, carry):
        base = pl.multiple_of(tg * W_GROUP, W_GROUP)
        ws = []
        for t in range(W_GROUP):
            n = base + t
            a = jnp.where(key == iks[pl.ds(n, 1), :], gs[pl.ds(n, 1), :], 0.0).astype(CDT)
            b = jnp.where(key == jks[pl.ds(n, 1), :], 1.0, 0.0).astype(CDT)
            ws.append(_nt(a, b))
        y = pltpu.einshape("tij->itj", jnp.stack(ws, axis=0))
        for i in range(N_KEYS):
            w_ref[pl.ds(base, W_GROUP), i * N_KEYS:(i + 1) * N_KEYS] = y[i].astype(w_ref.dtype)
        return carry

    lax.fori_loop(0, iks.shape[0] // W_GROUP, body, 0)


W_GROUP = 16


def peer_weights(ik, jk, g, tm=128):
    hk, N = ik.shape
    assert N % tm == 0 and tm % W_GROUP == 0
    blk = pl.BlockSpec((hk, tm), lambda i: (0, i))
    return pl.pallas_call(
        _peer_w_kernel,
        out_shape=jax.ShapeDtypeStruct((N, N_KEYS * N_KEYS), CDT),
        grid=(N // tm,), in_specs=[blk, blk, blk],
        out_specs=pl.BlockSpec((tm, N_KEYS * N_KEYS), lambda i: (i, 0)),
        scratch_shapes=[pltpu.VMEM((tm, hk), I32), pltpu.VMEM((tm, hk), I32), pltpu.VMEM((tm, hk), F32)],
        compiler_params=_cp(("parallel",)), name="peer_weights",
    )(ik, jk, g)


def _peer_dense_kernel(x_ref, u_ref, v_ref, w_ref, o_ref, a_ref):
    e = pl.program_id(1)

    @pl.when(e == 0)
    def _():
        o_ref[...] = jnp.zeros(o_ref.shape, F32)
        a_ref[...] = jnp.zeros(a_ref.shape, a_ref.dtype)

    slot = e % 2
    prev = a_ref[1 - slot]
    h = _nt(x_ref[...], u_ref[...].astype(CDT))
    a_ref[slot] = (jax.nn.gelu(h) * w_ref[...].astype(F32)).astype(CDT)
    o_ref[...] += _dot(prev, v_ref[...].astype(CDT))


def peer_dense(x, u, v, w, layer, tm=1024, te=512):
    N, D = x.shape
    E = u.shape[1]
    tm = tm if N % tm == 0 else N
    n_e = E // te
    return pl.pallas_call(
        _peer_dense_kernel,
        out_shape=jax.ShapeDtypeStruct((N, D), F32),
        grid=(N // tm, n_e + 1),
        in_specs=[pl.BlockSpec((tm, D), lambda i, e: (i, 0)),
                  pl.BlockSpec((None, te, D), lambda i, e: (layer, jnp.minimum(e, n_e - 1), 0)),
                  pl.BlockSpec((None, te, D), lambda i, e: (layer, jnp.maximum(e - 1, 0), 0)),
                  pl.BlockSpec((tm, te), lambda i, e: (i, jnp.minimum(e, n_e - 1)))],
        out_specs=pl.BlockSpec((tm, D), lambda i, e: (i, 0)),
        scratch_shapes=[pltpu.VMEM((2, tm, te), CDT)],
        compiler_params=_cp(("parallel", "arbitrary")), name="peer_dense",
    )(x, u, v, w)


def peer(xn2d, wq, subkeys, u, v, layer):
    q = matmul(xn2d, wq)
    ik, jk, g = peer_topk(q, subkeys)
    w = peer_weights(ik, jk, g)
    return peer_dense(xn2d, u, v, w, layer)


def _group_rows(mod, n_prompt_groups):
    w = mod.shape[1]
    return jnp.concatenate([jnp.broadcast_to(mod[0:1], (n_prompt_groups, w)), mod[1:]], axis=0)[:, None, :]


def nsa_mixer(xn, T, w_in, w_out, cmp_pe, cmp_w1, cmp_w2, rel_bias, pool, page_table, win_state):
    N = xn.shape[0]
    B, n_pages = page_table.shape
    n_pg = T // DEC_T
    nq, nkv = N_HEADS * HEAD_DIM, KV * HEAD_DIM
    n_pool, page = pool.shape[0], pool.shape[1]
    past = n_pages * page
    wb = win_state.shape[1]
    assert wb == WIN_A and page % CMP_STRIDE == 0 and page % SEL_BLOCK == 0 and page >= MAX_DISTANCE

    q = matmul(xn, w_in[:, :nq], out_dtype=CDT)
    rows = matmul(xn, w_in[:, nq:nq + 6 * nkv])
    w_gate = jnp.pad(w_in[:, nq + 6 * nkv:], ((0, 0), (0, LANE - N_BRANCH * N_HEADS)))
    gates = matmul(xn, w_gate, act_out="sigmoid")
    rows3 = rows.reshape(N // DEC_T, DEC_T, 6 * nkv)
    q_s = q[T:].astype(F32).reshape(B, DEC_T, nq)

    rows_bf_p = rows[:T].astype(CDT)
    kcvc = compress_prompt(rows[:T], cmp_pe, cmp_w1, cmp_w2)
    oc_p, sel_p = cmp_prompt(q, kcvc, rel_bias, n_q=T)
    tiles_sel = bias_tiles(rel_bias, 3, QB, QB, dstep=QB, keys_first=True)
    os_p = flash_prompt(q, rows_bf_p, 2, rows_bf_p[:, 3 * nkv:4 * nkv].T, tiles_sel, n_q=T, max_back=None, nsub=4,
                        sel_t=sel_p)
    tiles_w = bias_tiles(rel_bias, WIN_A // QB + 1, QB, QB, dstep=QB, window=WIN_A, keys_first=True)
    ow_p = flash_prompt(q, rows_bf_p, 4, rows_bf_p[:, 5 * nkv:6 * nkv].T, tiles_w, n_q=T, max_back=WIN_A // QB)

    pool3 = pool.reshape(n_pool, page * N_SLAB, HEAD_DIM)
    pool_p = pool_compress(page_table, pool3, cmp_pe, cmp_w1)
    n_cmp_s = (past + DEC_T - CMP_LEN) // CMP_STRIDE + 1
    tile_c = bias_tiles(rel_bias, 1, DEC_T, past // CMP_STRIDE, delta0=past, cs=CMP_STRIDE, coff=CMP_LEN - 1,
                        ncols_valid=n_cmp_s)
    oc_s, pen_s = cmp_sample(pool_p, q_s, cmp_w2, tile_c, page=page)
    tiles_past = bias_tiles(rel_bias, 2, DEC_T, page, delta0=2 * page, dstep=-page)
    tile_new = bias_tiles(rel_bias, 1, DEC_T, page)
    os_s = sel_sample(page_table, pool3, q_s, pen_s, rows3, tiles_past, tile_new, g0=n_pg)
    tile_buf = bias_tiles(rel_bias, 1, DEC_T, wb, delta0=wb, window=WIN_A)
    tile_wnew = bias_tiles(rel_bias, 1, DEC_T, LANE, window=WIN_A)
    ow_s, new_win_s = win_sample(q_s, win_state.reshape(B, wb * 2 * KV, HEAD_DIM), rows3, 2, tile_buf, tile_wnew,
                                 g0=n_pg)

    def both(p, s):
        return jnp.concatenate([p, s.reshape(B * DEC_T, nq)], axis=0)

    o = combine(gates, both(oc_p, oc_s), both(os_p, os_s), both(ow_p, ow_s))
    return matmul(o, w_out), rows, new_win_s


def swa_mixer(xn, T, kvsh, w_q, sinks, w_out, rel_bias, state):
    N = xn.shape[0]
    B, wb = state.shape[0], state.shape[1]
    assert wb == WIN_B
    nq, nkv = N_HEADS * HEAD_DIM, KV * HEAD_DIM
    q = matmul(xn, w_q, out_dtype=CDT)
    tiles = bias_tiles(rel_bias, WIN_B // QB + 1, QB, QB, dstep=QB, window=WIN_B, keys_first=True)
    kv_p = kvsh[:T].astype(CDT)
    o_p = flash_prompt(q, kv_p, 0, kv_p[:, nkv:].T, tiles, n_q=T, max_back=WIN_B // QB, sinks=sinks)
    tile_buf = bias_tiles(rel_bias, 1, DEC_T, wb, delta0=wb, window=WIN_B)
    tile_new = bias_tiles(rel_bias, 1, DEC_T, LANE, window=WIN_B)
    o_s, new_state = win_sample(q[T:].astype(F32).reshape(B, DEC_T, nq), state.reshape(B, wb * 2 * KV, HEAD_DIM),
                                kvsh.reshape(N // DEC_T, DEC_T, 2 * nkv), 0, tile_buf, tile_new, g0=T // DEC_T,
                                sinks=sinks)
    o = jnp.concatenate([o_p, o_s.reshape(B * DEC_T, nq)], axis=0).astype(CDT)
    return matmul(o, w_out), new_state


def kernel(x_prompt, x_sample, cache_nsa_kv, state_nsa_win, state_swa_kv, page_table, c_prompt, c_sample,
           ada_w, ada_b, norm_mix, norm_ffn, nsa_w_in, nsa_w_out, nsa_cmp_pe, nsa_cmp_w1, nsa_cmp_w2,
           kv_ada_w, kv_ada_b, kv_norm, kv_w, swa_w_q, swa_sinks, swa_w_out, rel_bias,
           peer_wq, peer_subkeys, peer_u, peer_v, final_norm):
    assert x_prompt.shape[0] == 1 and x_sample.shape[1] == DEC_T
    T, D = x_prompt.shape[1], x_prompt.shape[2]
    B = x_sample.shape[0]
    assert T % QB == 0 and D == N_HEADS * HEAD_DIM
    n_pg = T // DEC_T
    NG = n_pg + B
    N = NG * DEC_T
    nq = N_HEADS * HEAD_DIM
    nkv = KV * HEAD_DIM
    wb_a, wb_b = state_nsa_win.shape[2], state_swa_kv.shape[1]
    assert state_nsa_win.shape[0] == 1 and cache_nsa_kv.shape[0] == 1 and ada_w.shape[0] == 2

    h = jnp.concatenate([x_prompt.reshape(T, D), x_sample.reshape(B * DEC_T, D)], axis=0).reshape(NG, DEC_T, D)
    c = jnp.concatenate([c_prompt, c_sample], axis=0)
    mods = [_group_rows(matmul(c, ada_w[i], ada_b[i], act_in="silu"), n_pg) for i in range(2)]
    mod_kv = _group_rows(matmul(c, kv_ada_w, kv_ada_b, act_in="silu"), n_pg)

    (xn,) = resmod(h, None, [(norm_mix[0], (mods[0], 0, 1))], [CDT], emit_h=False)
    out, rows, new_win_s = nsa_mixer(xn, T, nsa_w_in[0], nsa_w_out[0], nsa_cmp_pe[0], nsa_cmp_w1[0], nsa_cmp_w2[0],
                                     rel_bias, cache_nsa_kv[0], page_table, state_nsa_win[0])
    h, xn = resmod(h, (out.reshape(NG, DEC_T, D), (mods[0], 2)), [(norm_ffn[0], (mods[0], 3, 4))], [CDT], emit_h=True)
    ffn = peer(xn, peer_wq[0], peer_subkeys[0], peer_u, peer_v, 0)

    h, xn, xkv = resmod(h, (ffn.reshape(NG, DEC_T, D), (mods[0], 5)),
                        [(norm_mix[1], (mods[1], 0, 1)), (kv_norm, (mod_kv, 0, 1))], [CDT, CDT], emit_h=True)
    kvsh = matmul(xkv, kv_w)
    out, new_swa_s = swa_mixer(xn, T, kvsh, swa_w_q[0], swa_sinks[0], swa_w_out[0], rel_bias, state_swa_kv)
    h, xn = resmod(h, (out.reshape(NG, DEC_T, D), (mods[1], 2)), [(norm_ffn[1], (mods[1], 3, 4))], [CDT], emit_h=True)
    ffn = peer(xn, peer_wq[1], peer_subkeys[1], peer_u, peer_v, 1)
    (y2d,) = resmod(h, (ffn.reshape(NG, DEC_T, D), (mods[1], 5)), [(final_norm, None)], [F32], emit_h=False)

    rows_p, rows_s = rows[:T], rows[T:]
    return (y2d[:T].reshape(1, T, D),
            y2d[T:].reshape(B, DEC_T, D),
            rows_p[:, :4 * nkv].reshape(1, 1, T, 4, KV, HEAD_DIM),
            rows_s[:, :4 * nkv].reshape(1, B, DEC_T, 4, KV, HEAD_DIM),
            rows_p[T - min(WIN_A, T):, 4 * nkv:].reshape(1, 1, min(WIN_A, T), 2, KV, HEAD_DIM),
            new_win_s.reshape(1, B, wb_a, 2, KV, HEAD_DIM),
            kvsh[T - min(WIN_B, T):T].reshape(1, min(WIN_B, T), 2, KV, HEAD_DIM),
            new_swa_s.reshape(B, wb_b, 2, KV, HEAD_DIM))
```

```python
import functools
import math

import jax
import jax.numpy as jnp
from jax import lax
from jax.experimental import pallas as pl
from jax.experimental.pallas import tpu as pltpu

F32 = jnp.float32
I32 = jnp.int32
CDT = jnp.bfloat16

N_HEADS = 16
HEAD_DIM = 128
KV = 2
G = N_HEADS // KV
SCALE = HEAD_DIM ** -0.5
CMP_LEN = 32
CMP_STRIDE = 16
SEL_BLOCK = 64
SEL_TOPK = 16
N_LOCAL_FORCED = 2
WIN_A = 512
WIN_B = 128
N_BRANCH = 3
FORCED_SCORE = 1e9
NUM_BUCKETS = 32
MAX_EXACT = 16
MAX_DISTANCE = 128
PEER_HEADS = 8
N_KEYS = 128
PEER_TOPK = 16
EPS = 1e-6
SEL_SHIFT = SEL_BLOCK.bit_length() - 1
TOPK_SHIFT = PEER_TOPK.bit_length() - 1
QB = 128
DEC_T = 8
LANE = 128
NEG = -jnp.inf

VMEM_LIMIT = 56 * 1024 * 1024


def _cp(sem=None, vmem=VMEM_LIMIT):
    return pltpu.CompilerParams(dimension_semantics=sem, vmem_limit_bytes=vmem)


def _nt(a, b):
    return lax.dot_general(a, b, (((1,), (1,)), ((), ())), preferred_element_type=F32)


def _dot(a, b):
    return jnp.dot(a, b, preferred_element_type=F32)


def _iota(shape, axis):
    return lax.broadcasted_iota(I32, shape, axis)


def _mm_kernel(x_ref, w_ref, b_ref, o_ref, *, act_in, act_out):
    x = x_ref[...]
    if act_in == "silu":
        x = x.astype(F32)
        x = x * jax.nn.sigmoid(x)
    acc = _dot(x.astype(CDT), w_ref[...].astype(CDT)) + b_ref[...]
    if act_out == "sigmoid":
        acc = jax.nn.sigmoid(acc)
    o_ref[...] = acc.astype(o_ref.dtype)


def matmul(x, w, b=None, *, layer=None, act_in=None, act_out=None, out_dtype=F32, tm=512, tn=512):
    M, K = x.shape
    N = w.shape[-1]
    tm = tm if M % tm == 0 else M
    tn = tn if N % tn == 0 else N
    if b is None:
        b = jnp.zeros((N,), F32)
    b = b.reshape(1, N).astype(F32)
    if layer is None:
        w_spec = pl.BlockSpec((K, tn), lambda j, i: (0, j))
    else:
        w_spec = pl.BlockSpec((None, K, tn), lambda j, i: (layer, 0, j))
    return pl.pallas_call(
        functools.partial(_mm_kernel, act_in=act_in, act_out=act_out),
        out_shape=jax.ShapeDtypeStruct((M, N), out_dtype),
        grid=(N // tn, M // tm),
        in_specs=[pl.BlockSpec((tm, K), lambda j, i: (i, 0)),
                  w_spec,
                  pl.BlockSpec((1, tn), lambda j, i: (0, j))],
        out_specs=pl.BlockSpec((tm, tn), lambda j, i: (i, j)),
        compiler_params=_cp(("parallel", "parallel")),
        name="matmul",
    )(x, w, b)


def _resmod_kernel(*refs, has_res, mod_flags, emit_h):
    it = iter(refs)
    h = next(it)[...]
    if has_res:
        o_ref, g_ref = next(it), next(it)
        h = h + g_ref[...] * o_ref[...]
    mods = []
    for has_mod in mod_flags:
        gain = next(it)
        mods.append((gain, next(it), next(it)) if has_mod else (gain, None, None))
    if emit_h:
        next(it)[...] = h
    y = h * lax.rsqrt(jnp.mean(h * h, axis=-1, keepdims=True) + EPS)
    for gain, sh, sc in mods:
        z = y * gain[...]
        if sh is not None:
            z = z * (1.0 + sc[...]) + sh[...]
        out = next(it)
        out[...] = z.reshape(out.shape).astype(out.dtype)


def resmod(h, res, mods, out_dtypes, *, emit_h, gb=16):
    NG, R, D = h.shape
    gb = gb if NG % gb == 0 else NG
    tok = pl.BlockSpec((gb, R, D), lambda i: (i, 0, 0))
    tok2 = pl.BlockSpec((gb * R, D), lambda i: (i, 0))

    def chunk(c):
        return pl.BlockSpec((gb, 1, D), lambda i, c=c: (i, 0, c))

    args, specs = [h], [tok]
    if res is not None:
        out, (garr, gc) = res
        args += [out, garr]
        specs += [tok, chunk(gc)]
    flags = []
    for gain, m in mods:
        args.append(gain.reshape(1, 1, D))
        specs.append(pl.BlockSpec((1, 1, D), lambda i: (0, 0, 0)))
        flags.append(m is not None)
        if m is not None:
            marr, shc, scc = m
            args += [marr, marr]
            specs += [chunk(shc), chunk(scc)]
    out_shape, out_specs = [], []
    if emit_h:
        out_shape.append(jax.ShapeDtypeStruct(h.shape, F32))
        out_specs.append(tok)
    for dt in out_dtypes:
        out_shape.append(jax.ShapeDtypeStruct((NG * R, D), dt))
        out_specs.append(tok2)
    return pl.pallas_call(
        functools.partial(_resmod_kernel, has_res=res is not None, mod_flags=tuple(flags), emit_h=emit_h),
        out_shape=out_shape, grid=(NG // gb,), in_specs=specs, out_specs=out_specs,
        compiler_params=_cp(("parallel",)), name="resmod",
    )(*args)


def _t5_bucket(dist):
    n = jnp.maximum(dist, 0)
    nf = jnp.maximum(n, 1).astype(F32)
    large = MAX_EXACT + (jnp.log(nf / MAX_EXACT) / math.log(MAX_DISTANCE / MAX_EXACT)
                         * (NUM_BUCKETS - MAX_EXACT)).astype(I32)
    large = jnp.minimum(large, NUM_BUCKETS - 1)
    return jnp.where(n < MAX_EXACT, n, large)


def _bias_lookup(bucket, rb_ref, h):
    out = jnp.full(bucket.shape, rb_ref[NUM_BUCKETS - 1, h], F32)
    for b in range(NUM_BUCKETS - 1):
        out = jnp.where(bucket == b, rb_ref[b, h], out)
    return out


def _bias_tile_kernel(rb_ref, o_ref, *, delta0, dstep, cs, coff, window, ncols_valid, keys_first):
    ti = pl.program_id(0)
    h = pl.program_id(1)
    shape = o_ref.shape[-2:]
    i = _iota(shape, 1 if keys_first else 0)
    j = _iota(shape, 0 if keys_first else 1)
    dist = delta0 + ti * dstep + i - (j * cs + coff)
    ok = dist >= 0
    if window is not None:
        ok = ok & (dist <= window)
    if ncols_valid is not None:
        ok = ok & (j < ncols_valid)
    o_ref[0, 0] = jnp.where(ok, _bias_lookup(_t5_bucket(dist), rb_ref, h), NEG)


def bias_tiles(rel_bias, n_tiles, rows, cols, *, delta0=0, dstep=0, cs=1, coff=0, window=None, ncols_valid=None,
               keys_first=False):
    return pl.pallas_call(
        functools.partial(_bias_tile_kernel, delta0=delta0, dstep=dstep, cs=cs, coff=coff, window=window,
                          ncols_valid=ncols_valid, keys_first=keys_first),
        out_shape=jax.ShapeDtypeStruct((n_tiles, N_HEADS, rows, cols), F32),
        grid=(n_tiles, N_HEADS),
        in_specs=[pl.BlockSpec(memory_space=pltpu.SMEM)],
        out_specs=pl.BlockSpec((1, 1, rows, cols), lambda t, h: (t, h, 0, 0)),
        compiler_params=_cp(("parallel", "parallel")), name="bias_tiles",
    )(rel_bias)


def _softmax_rows(s):
    m = jnp.max(s, axis=-1, keepdims=True)
    m = jnp.where(m == NEG, 0.0, m)
    e = jnp.exp(s - m)
    z = jnp.sum(e, axis=-1, keepdims=True)
    return e / jnp.where(z > 0, z, 1.0)


def _flash_step(s, v, m, l, acc):
    gg, r, kk = s.shape
    m_new = jnp.maximum(m, jnp.max(s, axis=-1, keepdims=True))
    m_safe = jnp.where(m_new == NEG, 0.0, m_new)
    alpha = jnp.exp(m - m_safe)
    p = jnp.exp(s - m_safe)
    l = alpha * l + jnp.sum(p, axis=-1, keepdims=True)
    pv = _dot(p.reshape(gg * r, kk).astype(CDT), v).reshape(gg, r, HEAD_DIM)
    return m_new, l, alpha * acc + pv


def _flash_finish(m, l, acc, sinks):
    if sinks is not None:
        m_f = jnp.maximum(m, sinks)
        a = jnp.exp(m - m_f)
        l = l * a + jnp.exp(sinks - m_f)
        acc = acc * a
    return acc / jnp.where(l > 0, l, 1.0)


def _head_rows(q, kv):
    return jnp.concatenate([q[:, (kv * G + g) * HEAD_DIM:(kv * G + g + 1) * HEAD_DIM] for g in range(G)],
                           axis=0).astype(CDT)


def _sink_col(sink_ref, kv):
    return jnp.concatenate([jnp.full((1, 1, 1), sink_ref[kv * G + g], F32) for g in range(G)], axis=0)


def _flash_prompt_kernel(*refs, n_tiles, max_back, has_sel, has_sinks, nsub):
    it = iter(refs)
    q_ref, k_ref, vt_ref, t_ref = next(it), next(it), next(it), next(it)
    sel_ref = next(it) if has_sel else None
    sink_ref = next(it) if has_sinks else None
    o_ref, m_ref, l_ref, acc_ref = next(it), next(it), next(it), next(it)
    qi = pl.program_id(0)
    kc = nsub * QB
    c_lo = 0 if max_back is None else jnp.maximum(qi - max_back, 0) // nsub
    c_hi = qi // nsub
    for kv in range(KV):
        cols = slice(kv * HEAD_DIM, (kv + 1) * HEAD_DIM)
        m_ref[...] = jnp.full(m_ref.shape, NEG, F32)
        l_ref[...] = jnp.zeros(l_ref.shape, F32)
        acc_ref[...] = jnp.zeros(acc_ref.shape, F32)
        qs = [q_ref[:, (kv * G + g) * HEAD_DIM:(kv * G + g + 1) * HEAD_DIM] for g in range(G)]

        def body(c, carry, kv=kv, qs=qs, cols=cols):
            start = pl.multiple_of(c * kc, kc)
            ks = k_ref[pl.ds(start, kc), cols]
            vts = vt_ref[cols, pl.ds(start, kc)]
            tidx = []
            for j in range(nsub):
                d = qi - (c * nsub + j)
                dead = d < 0 if max_back is None else (d < 0) | (d > max_back)
                tidx.append(jnp.where(dead, n_tiles, jnp.minimum(d, n_tiles - 1)))
            if has_sel:
                nj = sel_ref.shape[1]
                blk = (kc // SEL_BLOCK) * c + (_iota((kc, nj), 0) >> SEL_SHIFT)
                hit = _dot((_iota((kc, nj), 1) == blk).astype(CDT), sel_ref[kv])
                pen = jnp.where(hit > 0.5, 0.0, NEG)
            for g in range(G):
                h = kv * G + g
                s = _nt(ks, qs[g]) * SCALE + jnp.concatenate([t_ref[tidx[j], h] for j in range(nsub)], axis=0)
                if has_sel:
                    s = s + pen
                m = m_ref[g]
                m_new = jnp.maximum(m, jnp.max(s, axis=0, keepdims=True))
                m_safe = jnp.where(m_new == NEG, 0.0, m_new)
                alpha = jnp.exp(m - m_safe)
                p = jnp.exp(s - m_safe)
                m_ref[g] = m_new
                l_ref[g] = alpha * l_ref[g] + jnp.sum(p, axis=0, keepdims=True)
                acc_ref[g] = alpha * acc_ref[g] + _dot(vts, p.astype(CDT))
            return carry

        lax.fori_loop(c_lo, c_hi + 1, body, 0)
        for g in range(G):
            h = kv * G + g
            m, l, acc = m_ref[g], l_ref[g], acc_ref[g]
            if has_sinks:
                m_f = jnp.maximum(m, sink_ref[h])
                a = jnp.exp(m - m_f)
                l = l * a + jnp.exp(sink_ref[h] - m_f)
                acc = acc * a
            o_ref[:, h * HEAD_DIM:(h + 1) * HEAD_DIM] = (acc / jnp.where(l > 0, l, 1.0)).T


def flash_prompt(q, k_arr, kblk, v_t, tiles_t, *, n_q, max_back, sel_t=None, sinks=None, nsub=2):
    T = k_arr.shape[0]
    n_tiles = tiles_t.shape[0]
    assert T % (nsub * QB) == 0
    tiles_t = jnp.concatenate([tiles_t, jnp.full((1,) + tiles_t.shape[1:], NEG, F32)], axis=0)
    w = KV * HEAD_DIM
    args = [q, k_arr, v_t, tiles_t]
    specs = [pl.BlockSpec((QB, N_HEADS * HEAD_DIM), lambda i: (i, 0)),
             pl.BlockSpec((T, w), lambda i: (0, kblk)),
             pl.BlockSpec((w, T), lambda i: (0, 0)),
             pl.BlockSpec(tiles_t.shape, lambda i: (0, 0, 0, 0))]
    scratch = [pltpu.VMEM((G, 1, QB), F32), pltpu.VMEM((G, 1, QB), F32), pltpu.VMEM((G, HEAD_DIM, QB), F32)]
    if sel_t is not None:
        args.append(sel_t)
        specs.append(pl.BlockSpec((KV, sel_t.shape[1], QB), lambda i: (0, 0, i)))
    if sinks is not None:
        args.append(sinks)
        specs.append(pl.BlockSpec(memory_space=pltpu.SMEM))
    return pl.pallas_call(
        functools.partial(_flash_prompt_kernel, n_tiles=n_tiles, max_back=max_back, has_sel=sel_t is not None,
                          has_sinks=sinks is not None, nsub=nsub),
        out_shape=jax.ShapeDtypeStruct((n_q, N_HEADS * HEAD_DIM), F32),
        grid=(n_q // QB,), in_specs=specs,
        out_specs=pl.BlockSpec((QB, N_HEADS * HEAD_DIM), lambda i: (i, 0)),
        scratch_shapes=scratch,
        compiler_params=_cp(("parallel",)), name="flash_prompt",
    )(*args)


def _compress_rows(x, pe_ref, w1_ref, acc_ref, first):
    @pl.when(first)
    def _():
        acc_ref[...] = jnp.zeros(acc_ref.shape, F32)

    for gi in range(2 * KV):
        comp = gi // KV
        xg = x[:, gi * HEAD_DIM:(gi + 1) * HEAD_DIM]
        for a in range(2):
            xa = (xg + pe_ref[comp, a, 0]).astype(CDT)
            col = slice((gi * 2 + a) * HEAD_DIM, (gi * 2 + a + 1) * HEAD_DIM)
            acc_ref[:, col] += _dot(xa, w1_ref[comp, a, 0].astype(CDT))


def _compress_finish(p, w2_ref, gi):
    c = p.shape[0]
    pa = p[:, (gi * 2) * HEAD_DIM:(gi * 2 + 1) * HEAD_DIM]
    pb = p[:, (gi * 2 + 1) * HEAD_DIM:(gi * 2 + 2) * HEAD_DIM]
    h1 = jax.nn.gelu(pa + pltpu.roll(pb, c - 1, 0))
    return _dot(h1.astype(CDT), w2_ref[gi // KV].astype(CDT))


def _compress_prompt_kernel(x_ref, pe_ref, w1_ref, w2_ref, o_ref, acc_ref):
    b = pl.program_id(0)
    _compress_rows(x_ref[...], pe_ref, w1_ref, acc_ref, b == 0)

    @pl.when(b == CMP_STRIDE - 1)
    def _():
        p = acc_ref[...]
        for gi in range(2 * KV):
            o_ref[gi] = _compress_finish(p, w2_ref, gi)


def _cmp_weights(pe, w1):
    pe5 = pe.reshape(2, 2, CMP_STRIDE, 1, HEAD_DIM)
    w15 = w1.reshape(2, 2, CMP_STRIDE, HEAD_DIM, HEAD_DIM)
    return pe5, w15


def compress_prompt(rows, pe, w1, w2):
    T = rows.shape[0]
    C = T // CMP_STRIDE
    x = rows.reshape(C, CMP_STRIDE * rows.shape[1])
    nblk = rows.shape[1] // (4 * HEAD_DIM)
    pe5, w15 = _cmp_weights(pe, w1)
    return pl.pallas_call(
        _compress_prompt_kernel,
        out_shape=jax.ShapeDtypeStruct((2 * KV, C, HEAD_DIM), F32),
        grid=(CMP_STRIDE,),
        in_specs=[pl.BlockSpec((C, 4 * HEAD_DIM), lambda b: (0, b * nblk)),
                  pl.BlockSpec((2, 2, 1, 1, HEAD_DIM), lambda b: (0, 0, b, 0, 0)),
                  pl.BlockSpec((2, 2, 1, HEAD_DIM, HEAD_DIM), lambda b: (0, 0, b, 0, 0)),
                  pl.BlockSpec((2, HEAD_DIM, HEAD_DIM), lambda b: (0, 0, 0))],
        out_specs=pl.BlockSpec((2 * KV, C, HEAD_DIM), lambda b: (0, 0, 0)),
        scratch_shapes=[pltpu.VMEM((C, 8 * HEAD_DIM), F32)],
        compiler_params=_cp(("arbitrary",)), name="compress_prompt",
    )(x, pe5, w15, w2)


def _overlap(n_idx, j_idx, n_cmp, n_slc):
    return ((n_idx * CMP_STRIDE <= j_idx * SEL_BLOCK + SEL_BLOCK - 1)
            & (n_idx * CMP_STRIDE + CMP_LEN - 1 >= j_idx * SEL_BLOCK)
            & (n_idx < n_cmp) & (j_idx < n_slc))


def _split_hi_lo(x):
    hi = x.astype(CDT)
    return hi, (x - hi.astype(F32)).astype(CDT)


def _sel_scores(imp, j_idx, t, n_slc):
    valid = (j_idx * SEL_BLOCK <= t) & (j_idx < n_slc)
    cur = t >> SEL_SHIFT
    forced = (j_idx == 0) | ((j_idx <= cur) & (j_idx > cur - N_LOCAL_FORCED))
    return jnp.where(valid, jnp.where(forced, FORCED_SCORE, imp), NEG)


def _topk_mask(score, idx, axis, k):
    n = score.shape[axis]
    sel = jnp.zeros(score.shape, F32)
    for _ in range(k):
        m = jnp.max(score, axis=axis, keepdims=True)
        first = jnp.min(jnp.where(score == m, idx, n), axis=axis, keepdims=True)
        hit = idx == first
        sel = jnp.where(hit & (m > NEG), 1.0, sel)
        score = jnp.where(hit, NEG, score)
    return sel


def _cmp_prompt_kernel(q_ref, kc_ref, rb_ref, oc_ref, sel_ref, *, n_cmp, n_slc, k_top):
    qi = pl.program_id(0)
    t0 = qi * QB
    C = kc_ref.shape[1]
    J = sel_ref.shape[1]
    nw = min(LANE, C)
    per_q = QB // CMP_STRIDE
    assert CMP_STRIDE * nw - (QB + CMP_LEN - 1) + CMP_STRIDE >= MAX_DISTANCE
    nb = pl.multiple_of(jnp.clip(per_q * (qi + 1) - nw, 0, C - nw), 8)
    far_ok = _iota((QB, C), 1) < nb
    n_near = nb + _iota((QB, nw), 1)
    dist = t0 + _iota((QB, nw), 0) - (n_near * CMP_STRIDE + CMP_LEN - 1)
    near_ok = (dist >= 0) & (n_near < n_cmp)
    bucket = _t5_bucket(dist)
    ov_far = _overlap(_iota((J, C), 1), _iota((J, C), 0), n_cmp, n_slc).astype(CDT)
    ov_near = _overlap(nb + _iota((J, nw), 1), _iota((J, nw), 0), n_cmp, n_slc).astype(CDT)
    j_t = _iota((J, QB), 0)
    t_t = t0 + _iota((J, QB), 1)
    q = q_ref[...]
    for kv in range(KV):
        q2 = _head_rows(q, kv)
        far_bias = jnp.concatenate([jnp.full((1, 1, 1), rb_ref[NUM_BUCKETS - 1, kv * G + g], F32) for g in range(G)],
                                   axis=0)
        s_far = _nt(q2, kc_ref[kv].astype(CDT)).reshape(G, QB, C) * SCALE + far_bias
        s_far = jnp.where(far_ok[None], s_far, NEG)
        bias = jnp.stack([_bias_lookup(bucket, rb_ref, kv * G + g) for g in range(G)], axis=0)
        s_near = _nt(q2, kc_ref[kv, pl.ds(nb, nw), :].astype(CDT)).reshape(G, QB, nw) * SCALE + bias
        s_near = jnp.where(near_ok[None], s_near, NEG)
        m = jnp.maximum(jnp.max(s_far, axis=-1, keepdims=True), jnp.max(s_near, axis=-1, keepdims=True))
        m = jnp.where(m == NEG, 0.0, m)
        e_far = jnp.exp(s_far - m)
        e_near = jnp.exp(s_near - m)
        z = jnp.sum(e_far, axis=-1, keepdims=True) + jnp.sum(e_near, axis=-1, keepdims=True)
        z = jnp.where(z > 0, z, 1.0)
        p_far = e_far / z
        p_near = e_near / z
        o = (_dot(p_far.reshape(G * QB, C).astype(CDT), kc_ref[KV + kv].astype(CDT))
             + _dot(p_near.reshape(G * QB, nw).astype(CDT), kc_ref[KV + kv, pl.ds(nb, nw), :].astype(CDT)))
        o = o.reshape(G, QB, HEAD_DIM)
        for g in range(G):
            oc_ref[:, (kv * G + g) * HEAD_DIM:(kv * G + g + 1) * HEAD_DIM] = o[g]
        hi_f, lo_f = _split_hi_lo(jnp.sum(p_far, axis=0))
        hi_n, lo_n = _split_hi_lo(jnp.sum(p_near, axis=0))
        imp_t = _nt(ov_far, hi_f) + _nt(ov_far, lo_f) + _nt(ov_near, hi_n) + _nt(ov_near, lo_n)
        sel_t = _topk_mask(_sel_scores(imp_t, j_t, t_t, n_slc), j_t, 0, k_top)
        sel_ref[kv] = sel_t.astype(sel_ref.dtype)


def cmp_prompt(q, kcvc, rel_bias, *, n_q):
    C = kcvc.shape[1]
    n_cmp = (n_q - CMP_LEN) // CMP_STRIDE + 1
    n_slc = -(-n_q // SEL_BLOCK)
    return pl.pallas_call(
        functools.partial(_cmp_prompt_kernel, n_cmp=n_cmp, n_slc=n_slc, k_top=min(SEL_TOPK, n_slc)),
        out_shape=[jax.ShapeDtypeStruct((n_q, N_HEADS * HEAD_DIM), F32),
                   jax.ShapeDtypeStruct((KV, n_slc, n_q), CDT)],
        grid=(n_q // QB,),
        in_specs=[pl.BlockSpec((QB, N_HEADS * HEAD_DIM), lambda i: (i, 0)),
                  pl.BlockSpec(kcvc.shape, lambda i: (0, 0, 0)),
                  pl.BlockSpec(memory_space=pltpu.SMEM)],
        out_specs=[pl.BlockSpec((QB, N_HEADS * HEAD_DIM), lambda i: (i, 0)),
                   pl.BlockSpec((KV, n_slc, QB), lambda i: (0, 0, i))],
        compiler_params=_cp(("parallel",)), name="cmp_prompt",
    )(q, kcvc, rel_bias)


def _combine_kernel(g_ref, c_ref, s_ref, w_ref, o_ref):
    gates = g_ref[...]
    for h in range(N_HEADS):
        col = slice(h * HEAD_DIM, (h + 1) * HEAD_DIM)
        o = (gates[:, N_BRANCH * h:N_BRANCH * h + 1] * c_ref[:, col]
             + gates[:, N_BRANCH * h + 1:N_BRANCH * h + 2] * s_ref[:, col]
             + gates[:, N_BRANCH * h + 2:N_BRANCH * h + 3] * w_ref[:, col])
        o_ref[:, col] = o.astype(o_ref.dtype)


def combine(gates, oc, os_, ow, tm=256):
    N, D = oc.shape
    assert N % tm == 0
    tok = pl.BlockSpec((tm, D), lambda i: (i, 0))
    return pl.pallas_call(
        _combine_kernel, out_shape=jax.ShapeDtypeStruct((N, D), CDT), grid=(N // tm,),
        in_specs=[pl.BlockSpec((tm, gates.shape[1]), lambda i: (i, 0)), tok, tok, tok], out_specs=tok,
        compiler_params=_cp(("parallel",)), name="nsa_combine",
    )(gates, oc, os_, ow)


N_SLAB = 4 * KV


def _page_slab(p_ref, slab, row0, n, step):
    return p_ref[0, pl.ds(row0 * N_SLAB + slab, n, stride=step * N_SLAB), :]


def _pool_cmp_kernel(pt_ref, *refs, n_ops, cpp):
    p_refs = refs[:n_ops]
    pe_ref, w1_ref, o_ref = refs[n_ops:]
    nc = n_ops * cpp
    page = cpp * CMP_STRIDE
    for comp in range(2):
        parts = []
        for kv in range(KV):
            by_b = [pltpu.einshape("cbd->bcd", _page_slab(p, comp * KV + kv, 0, page, 1).reshape(
                cpp, CMP_STRIDE, HEAD_DIM)) for p in p_refs]
            parts.append(jnp.concatenate([jnp.concatenate([y[b] for y in by_b], axis=0) for b in range(CMP_STRIDE)],
                                         axis=1))
        x = jnp.concatenate(parts, axis=0)
        for a in range(2):
            d = _dot((x + pe_ref[comp, a]).astype(CDT), w1_ref[comp, a])
            for kv in range(KV):
                col = ((comp * KV + kv) * 2 + a) * HEAD_DIM
                o_ref[:, col:col + HEAD_DIM] = d[kv * nc:(kv + 1) * nc]


def pool_compress(page_table, pool3, pe, w1, n_ops=16):
    B, n_pages = page_table.shape
    page = pool3.shape[1] // N_SLAB
    cpp = page // CMP_STRIDE
    n_ops = n_ops if n_pages % n_ops == 0 else n_pages
    ng = n_pages // n_ops
    half = CMP_STRIDE * HEAD_DIM
    pe4 = pe.reshape(2, 2, 1, half)
    w14 = w1.reshape(2, 2, half, HEAD_DIM).astype(CDT)
    p_specs = [pl.BlockSpec((1, page * N_SLAB, HEAD_DIM), lambda b, g, pt, r=r: (pt[b, g * n_ops + r], 0, 0))
               for r in range(n_ops)]
    grid_spec = pltpu.PrefetchScalarGridSpec(
        num_scalar_prefetch=1, grid=(B, ng),
        in_specs=p_specs + [pl.BlockSpec(pe4.shape, lambda b, g, pt: (0, 0, 0, 0)),
                            pl.BlockSpec(w14.shape, lambda b, g, pt: (0, 0, 0, 0))],
        out_specs=pl.BlockSpec((n_ops * cpp, 8 * HEAD_DIM), lambda b, g, pt: (b * ng + g, 0)))
    return pl.pallas_call(
        functools.partial(_pool_cmp_kernel, n_ops=n_ops, cpp=cpp),
        out_shape=jax.ShapeDtypeStruct((B * n_pages * cpp, 8 * HEAD_DIM), F32),
        grid_spec=grid_spec, compiler_params=_cp(("parallel", "arbitrary")), name="pool_compress",
    )(page_table, *([pool3] * n_ops), pe4, w14)


def _topk_rank_mask(score, k, n_valid):
    nr, nj = score.shape
    jr = -(-n_valid // 8) * 8
    st = jnp.concatenate([score, jnp.full((LANE - nr, nj), NEG, F32)], axis=0).T
    idx_c = _iota((jr, nj), 0)
    idx_r = _iota((jr, nj), 1)
    rows = []
    for r in range(nr):
        col = st[:jr, r:r + 1]
        row = score[r:r + 1, :]
        beats = (col > row) | ((col == row) & (idx_c < idx_r))
        rank = jnp.sum(jnp.where(beats, 1.0, 0.0), axis=0, keepdims=True)
        rows.append(jnp.where((rank < k) & (row > NEG), 1.0, 0.0))
    return jnp.concatenate(rows, axis=0)


def _cmp_sample_kernel(p_ref, q_ref, w2_ref, t_ref, e_ref, oc_ref, pen_ref, *, n_cmp, n_slc, k_top, q_pos0):
    p = p_ref[...]
    C = p.shape[0]
    J = e_ref.shape[0]
    q = q_ref[0]
    ov = _overlap(_iota((C, J), 0), _iota((C, J), 1), n_cmp, n_slc).astype(CDT)
    j_idx = _iota((DEC_T, J), 1)
    t = q_pos0 + _iota((DEC_T, J), 0)
    scores = []
    for kv in range(KV):
        kc = _compress_finish(p, w2_ref, kv).astype(CDT)
        vc = _compress_finish(p, w2_ref, KV + kv).astype(CDT)
        q2 = _head_rows(q, kv)
        s = _nt(q2, kc).reshape(G, DEC_T, C) * SCALE + t_ref[0, kv * G:(kv + 1) * G]
        pr = _softmax_rows(s)
        o = _dot(pr.reshape(G * DEC_T, C).astype(CDT), vc)
        for g in range(G):
            oc_ref[0, :, (kv * G + g) * HEAD_DIM:(kv * G + g + 1) * HEAD_DIM] = o[g * DEC_T:(g + 1) * DEC_T]
        hi, lo = _split_hi_lo(jnp.sum(pr, axis=0))
        imp = _dot(hi, ov) + _dot(lo, ov)
        scores.append(_sel_scores(imp, j_idx, t, n_slc))
    sel = _topk_rank_mask(jnp.concatenate(scores, axis=0), k_top, n_slc)
    pen_ref[0] = jnp.where(_dot(sel.astype(CDT), e_ref[...]) > 0.5, 0.0, NEG)


def cmp_sample(pool_p, q3, w2, tile_c, *, page):
    B = q3.shape[0]
    C = pool_p.shape[0] // B
    past = C * CMP_STRIDE
    L = past + DEC_T
    n_cmp = (L - CMP_LEN) // CMP_STRIDE + 1
    n_slc = -(-L // SEL_BLOCK)
    jp = -(-n_slc // LANE) * LANE
    D = N_HEADS * HEAD_DIM
    nk = past + page
    e_mat = (jnp.arange(jp, dtype=I32)[:, None] == jnp.arange(nk, dtype=I32)[None, :] // SEL_BLOCK).astype(CDT)
    return pl.pallas_call(
        functools.partial(_cmp_sample_kernel, n_cmp=n_cmp, n_slc=n_slc, k_top=min(SEL_TOPK, n_slc), q_pos0=past),
        out_shape=[jax.ShapeDtypeStruct((B, DEC_T, D), F32), jax.ShapeDtypeStruct((B, KV * DEC_T, nk), F32)],
        grid=(B,),
        in_specs=[pl.BlockSpec((C, 8 * HEAD_DIM), lambda b: (b, 0)),
                  pl.BlockSpec((1, DEC_T, D), lambda b: (b, 0, 0)),
                  pl.BlockSpec(w2.shape, lambda b: (0, 0, 0)),
                  pl.BlockSpec(tile_c.shape, lambda b: (0, 0, 0, 0)),
                  pl.BlockSpec((jp, nk), lambda b: (0, 0))],
        out_specs=[pl.BlockSpec((1, DEC_T, D), lambda b: (b, 0, 0)),
                   pl.BlockSpec((1, KV * DEC_T, nk), lambda b: (b, 0, 0))],
        compiler_params=_cp(("parallel",)), name="cmp_sample",
    )(pool_p, q3, w2, tile_c, e_mat)


def _sel_sample_kernel(pt_ref, *refs, n_ops, n_pages):
    p_refs = refs[:n_ops]
    q_ref, pen_ref, new_ref, tp_ref, tn_ref, o_ref, m_ref, l_ref, acc_ref = refs[n_ops:]
    gstep = pl.program_id(1)
    page = p_refs[0].shape[1] // N_SLAB
    nk = n_ops * page

    @pl.when(gstep == 0)
    def _():
        m_ref[...] = jnp.full(m_ref.shape, NEG, F32)
        l_ref[...] = jnp.zeros(l_ref.shape, F32)
        acc_ref[...] = jnp.zeros(acc_ref.shape, F32)

    q = q_ref[0]

    def update(kv, k, v, bias, pen):
        s = _nt(_head_rows(q, kv), k).reshape(G, DEC_T, k.shape[0]) * SCALE + bias + pen[None]
        m, l, acc = _flash_step(s, v, m_ref[kv], l_ref[kv], acc_ref[kv])
        m_ref[kv] = m
        l_ref[kv] = l
        acc_ref[kv] = acc

    for kv in range(KV):
        k = jnp.concatenate([_page_slab(p, 2 * KV + kv, 0, page, 1) for p in p_refs], axis=0).astype(CDT)
        v = jnp.concatenate([_page_slab(p, 3 * KV + kv, 0, page, 1) for p in p_refs], axis=0).astype(CDT)
        bias = jnp.concatenate(
            [tp_ref[jnp.where(gstep * n_ops + r == n_pages - 1, 1, 0), kv * G:(kv + 1) * G] for r in range(n_ops)],
            axis=-1)
        pen = pen_ref[0, kv * DEC_T:(kv + 1) * DEC_T, pl.ds(pl.multiple_of(gstep * nk, nk), nk)]
        update(kv, k, v, bias, pen)

    @pl.when(gstep == pl.num_programs(1) - 1)
    def _():
        new = new_ref[0]
        pad = jnp.zeros((page - DEC_T, HEAD_DIM), F32)
        past = n_pages * page
        for kv in range(KV):
            k = jnp.concatenate([new[:, kv * HEAD_DIM:(kv + 1) * HEAD_DIM], pad], axis=0).astype(CDT)
            v = jnp.concatenate([new[:, (KV + kv) * HEAD_DIM:(KV + kv + 1) * HEAD_DIM], pad], axis=0).astype(CDT)
            update(kv, k, v, tn_ref[0, kv * G:(kv + 1) * G], pen_ref[0, kv * DEC_T:(kv + 1) * DEC_T, past:past + page])
            o = _flash_finish(m_ref[kv], l_ref[kv], acc_ref[kv], None)
            for g in range(G):
                o_ref[0, :, (kv * G + g) * HEAD_DIM:(kv * G + g + 1) * HEAD_DIM] = o[g]


def sel_sample(page_table, pool3, q3, pen, rows3, tiles_past, tile_new, *, g0, n_ops=16):
    B, n_pages = page_table.shape
    page = pool3.shape[1] // N_SLAB
    n_ops = n_ops if n_pages % n_ops == 0 else n_pages
    D = N_HEADS * HEAD_DIM
    w = 2 * KV * HEAD_DIM
    p_specs = [pl.BlockSpec((1, page * N_SLAB, HEAD_DIM), lambda b, g, pt, r=r: (pt[b, g * n_ops + r], 0, 0))
               for r in range(n_ops)]
    grid_spec = pltpu.PrefetchScalarGridSpec(
        num_scalar_prefetch=1, grid=(B, n_pages // n_ops),
        in_specs=p_specs + [pl.BlockSpec((1, DEC_T, D), lambda b, g, pt: (b, 0, 0)),
                            pl.BlockSpec((1,) + pen.shape[1:], lambda b, g, pt: (b, 0, 0)),
                            pl.BlockSpec((1, DEC_T, w), lambda b, g, pt: (g0 + b, 0, 1)),
                            pl.BlockSpec(tiles_past.shape, lambda b, g, pt: (0, 0, 0, 0)),
                            pl.BlockSpec(tile_new.shape, lambda b, g, pt: (0, 0, 0, 0))],
        out_specs=pl.BlockSpec((1, DEC_T, D), lambda b, g, pt: (b, 0, 0)),
        scratch_shapes=[pltpu.VMEM((KV, G, DEC_T, 1), F32), pltpu.VMEM((KV, G, DEC_T, 1), F32),
                        pltpu.VMEM((KV, G, DEC_T, HEAD_DIM), F32)])
    return pl.pallas_call(
        functools.partial(_sel_sample_kernel, n_ops=n_ops, n_pages=n_pages),
        out_shape=jax.ShapeDtypeStruct((B, DEC_T, D), F32),
        grid_spec=grid_spec, compiler_params=_cp(("parallel", "arbitrary")), name="sel_sample",
    )(page_table, *([pool3] * n_ops), q3, pen, rows3, tiles_past, tile_new)


def _win_sample_kernel(*refs, has_sinks):
    it = iter(refs)
    q_ref, buf_ref, new_ref, tb_ref, tn_ref = next(it), next(it), next(it), next(it), next(it)
    sink_ref = next(it) if has_sinks else None
    o_ref, st_ref = next(it), next(it)
    q = q_ref[0]
    new = new_ref[0]
    ns = 2 * KV
    wb = buf_ref.shape[1] // ns
    npad = tn_ref.shape[-1]
    pad = jnp.zeros((npad - DEC_T, HEAD_DIM), F32)
    for kv in range(KV):
        kc = slice(kv * HEAD_DIM, (kv + 1) * HEAD_DIM)
        vc = slice((KV + kv) * HEAD_DIM, (KV + kv + 1) * HEAD_DIM)
        k = jnp.concatenate([buf_ref[0, pl.ds(kv, wb, stride=ns), :], new[:, kc], pad], axis=0).astype(CDT)
        v = jnp.concatenate([buf_ref[0, pl.ds(KV + kv, wb, stride=ns), :], new[:, vc], pad], axis=0).astype(CDT)
        bias = jnp.concatenate([tb_ref[0, kv * G:(kv + 1) * G], tn_ref[0, kv * G:(kv + 1) * G]], axis=-1)
        s = _nt(_head_rows(q, kv), k).reshape(G, DEC_T, wb + npad) * SCALE + bias
        init = (jnp.full((G, DEC_T, 1), NEG, F32), jnp.zeros((G, DEC_T, 1), F32), jnp.zeros((G, DEC_T, HEAD_DIM), F32))
        m, l, acc = _flash_step(s, v, *init)
        o = _flash_finish(m, l, acc, _sink_col(sink_ref, kv) if has_sinks else None)
        for g in range(G):
            o_ref[0, :, (kv * G + g) * HEAD_DIM:(kv * G + g + 1) * HEAD_DIM] = o[g]
    st_ref[0, :(wb - DEC_T) * ns, :] = buf_ref[0, DEC_T * ns:, :]
    for s in range(ns):
        st_ref[0, pl.ds((wb - DEC_T) * ns + s, DEC_T, stride=ns), :] = new[:, s * HEAD_DIM:(s + 1) * HEAD_DIM]


def win_sample(q3, buf, new3, new_blk, tile_buf, tile_new, *, g0, sinks=None):
    B, nrow = buf.shape[0], buf.shape[1]
    w = 2 * KV * HEAD_DIM
    D = N_HEADS * HEAD_DIM
    args = [q3, buf, new3, tile_buf, tile_new]
    specs = [pl.BlockSpec((1, DEC_T, D), lambda b: (b, 0, 0)),
             pl.BlockSpec((1, nrow, HEAD_DIM), lambda b: (b, 0, 0)),
             pl.BlockSpec((1, DEC_T, w), lambda b: (g0 + b, 0, new_blk)),
             pl.BlockSpec(tile_buf.shape, lambda b: (0, 0, 0, 0)),
             pl.BlockSpec(tile_new.shape, lambda b: (0, 0, 0, 0))]
    if sinks is not None:
        args.append(sinks)
        specs.append(pl.BlockSpec(memory_space=pltpu.SMEM))
    return pl.pallas_call(
        functools.partial(_win_sample_kernel, has_sinks=sinks is not None),
        out_shape=[jax.ShapeDtypeStruct((B, DEC_T, D), F32), jax.ShapeDtypeStruct(buf.shape, F32)],
        grid=(B,), in_specs=specs,
        out_specs=[pl.BlockSpec((1, DEC_T, D), lambda b: (b, 0, 0)),
                   pl.BlockSpec((1, nrow, HEAD_DIM), lambda b: (b, 0, 0))],
        compiler_params=_cp(("parallel",)), name="win_sample",
    )(*args)


def _topk_rows(s, k):
    r = s.shape[0]
    idx = _iota(s.shape, 0)
    vals, ids = [], []
    for _ in range(k):
        m = jnp.max(s, axis=0, keepdims=True)
        first = jnp.min(jnp.where(s == m, idx, r), axis=0, keepdims=True)
        vals.append(m)
        ids.append(first)
        s = jnp.where(idx == first, NEG, s)
    return jnp.concatenate(vals, axis=0), jnp.concatenate(ids, axis=0)


def _peer_topk_kernel(q_ref, sk_ref, ik_ref, jk_ref, g_ref):
    half = sk_ref.shape[-1]
    sv, si = [], []
    for c in range(2):
        s_t = _nt(sk_ref[0, c].astype(CDT), q_ref[:, c * half:(c + 1) * half].astype(CDT))
        v, i = _topk_rows(s_t, PEER_TOPK)
        sv.append(v)
        si.append(i)
    blocks, starts, offs = [], [], []
    r0 = 0
    for a in range(PEER_TOPK):
        nb = PEER_TOPK // (a + 1)
        blocks.append(sv[0][a:a + 1] + sv[1][:nb])
        starts.append(r0)
        offs.append(a * PEER_TOPK - r0)
        r0 += nb
    blocks.append(jnp.full((-r0 % 8, sv[0].shape[1]), NEG, F32))
    top, cr = _topk_rows(jnp.concatenate(blocks, axis=0), PEER_TOPK)
    ci = cr + offs[0]
    for a in range(1, PEER_TOPK):
        ci = jnp.where(cr >= starts[a], cr + offs[a], ci)
    a_idx = ci >> TOPK_SHIFT
    b_idx = ci & (PEER_TOPK - 1)
    isel = jnp.zeros(ci.shape, I32)
    jsel = jnp.zeros(ci.shape, I32)
    for a in range(PEER_TOPK):
        isel = jnp.where(a_idx == a, si[0][a:a + 1], isel)
        jsel = jnp.where(b_idx == a, si[1][a:a + 1], jsel)
    e = jnp.exp(top - top[0:1])
    ik_ref[...] = isel
    jk_ref[...] = jsel
    g_ref[...] = e / jnp.sum(e, axis=0, keepdims=True)


def peer_topk(q, subkeys, tm=256):
    N = q.shape[0]
    assert N % tm == 0
    half = subkeys.shape[-1]
    outs = pl.BlockSpec((PEER_TOPK, tm), lambda i, h: (h, i))
    shp = (PEER_HEADS * PEER_TOPK, N)
    return pl.pallas_call(
        _peer_topk_kernel,
        out_shape=[jax.ShapeDtypeStruct(shp, I32), jax.ShapeDtypeStruct(shp, I32), jax.ShapeDtypeStruct(shp, F32)],
        grid=(N // tm, PEER_HEADS),
        in_specs=[pl.BlockSpec((tm, 2 * half), lambda i, h: (i, h)),
                  pl.BlockSpec((1, 2, N_KEYS, half), lambda i, h: (h, 0, 0, 0))],
        out_specs=[outs, outs, outs],
        compiler_params=_cp(("parallel", "parallel")), name="peer_topk",
    )(q, subkeys)


def _peer_w_kernel(ik_ref, jk_ref, g_ref, w_ref, iks, jks, gs):
    iks[...] = ik_ref[...].T
    jks[...] = jk_ref[...].T
    gs[...] = g_ref[...].T
    hk = iks.shape[1]
    key = _iota((N_KEYS, hk), 0)

    def body(tg, carry):
        base = pl.multiple_of(tg * W_GROUP, W_GROUP)
        ws = []
        for t in range(W_GROUP):
            n = base + t
            a = jnp.where(key == iks[pl.ds(n, 1), :], gs[pl.ds(n, 1), :], 0.0).astype(CDT)
            b = jnp.where(key == jks[pl.ds(n, 1), :], 1.0, 0.0).astype(CDT)
            ws.append(_nt(a, b))
        y = pltpu.einshape("tij->itj", jnp.stack(ws, axis=0))
        for i in range(N_KEYS):
            w_ref[pl.ds(base, W_GROUP), i * N_KEYS:(i + 1) * N_KEYS] = y[i].astype(w_ref.dtype)
        return carry

    lax.fori_loop(0, iks.shape[0] // W_GROUP, body, 0)


W_GROUP = 16


def peer_weights(ik, jk, g, tm=128):
    hk, N = ik.shape
    assert N % tm == 0 and tm % W_GROUP == 0
    blk = pl.BlockSpec((hk, tm), lambda i: (0, i))
    return pl.pallas_call(
        _peer_w_kernel,
        out_shape=jax.ShapeDtypeStruct((N, N_KEYS * N_KEYS), CDT),
        grid=(N // tm,), in_specs=[blk, blk, blk],
        out_specs=pl.BlockSpec((tm, N_KEYS * N_KEYS), lambda i: (i, 0)),
        scratch_shapes=[pltpu.VMEM((tm, hk), I32), pltpu.VMEM((tm, hk), I32), pltpu.VMEM((tm, hk), F32)],
        compiler_params=_cp(("parallel",)), name="peer_weights",
    )(ik, jk, g)


def _peer_dense_kernel(x_ref, u_ref, v_ref, w_ref, o_ref, a_ref):
    e = pl.program_id(1)

    @pl.when(e == 0)
    def _():
        o_ref[...] = jnp.zeros(o_ref.shape, F32)
        a_ref[...] = jnp.zeros(a_ref.shape, a_ref.dtype)

    slot = e % 2
    prev = a_ref[1 - slot]
    h = _nt(x_ref[...], u_ref[...].astype(CDT))
    a_ref[slot] = (jax.nn.gelu(h) * w_ref[...].astype(F32)).astype(CDT)
    o_ref[...] += _dot(prev, v_ref[...].astype(CDT))


def peer_dense(x, u, v, w, layer, tm=1024, te=512):
    N, D = x.shape
    E = u.shape[1]
    tm = tm if N % tm == 0 else N
    n_e = E // te
    return pl.pallas_call(
        _peer_dense_kernel,
        out_shape=jax.ShapeDtypeStruct((N, D), F32),
        grid=(N // tm, n_e + 1),
        in_specs=[pl.BlockSpec((tm, D), lambda i, e: (i, 0)),
                  pl.BlockSpec((None, te, D), lambda i, e: (layer, jnp.minimum(e, n_e - 1), 0)),
                  pl.BlockSpec((None, te, D), lambda i, e: (layer, jnp.maximum(e - 1, 0), 0)),
                  pl.BlockSpec((tm, te), lambda i, e: (i, jnp.minimum(e, n_e - 1)))],
        out_specs=pl.BlockSpec((tm, D), lambda i, e: (i, 0)),
        scratch_shapes=[pltpu.VMEM((2, tm, te), CDT)],
        compiler_params=_cp(("parallel", "arbitrary")), name="peer_dense",
    )(x, u, v, w)


def peer(xn2d, wq, subkeys, u, v, layer):
    q = matmul(xn2d, wq)
    ik, jk, g = peer_topk(q, subkeys)
    w = peer_weights(ik, jk, g)
    return peer_dense(xn2d, u, v, w, layer)


def _group_rows(mod, n_prompt_groups):
    w = mod.shape[1]
    return jnp.concatenate([jnp.broadcast_to(mod[0:1], (n_prompt_groups, w)), mod[1:]], axis=0)[:, None, :]


def nsa_mixer(xn, T, w_in, w_out, cmp_pe, cmp_w1, cmp_w2, rel_bias, pool, page_table, win_state):
    N = xn.shape[0]
    B, n_pages = page_table.shape
    n_pg = T // DEC_T
    nq, nkv = N_HEADS * HEAD_DIM, KV * HEAD_DIM
    n_pool, page = pool.shape[0], pool.shape[1]
    past = n_pages * page
    wb = win_state.shape[1]
    assert wb == WIN_A and page % CMP_STRIDE == 0 and page % SEL_BLOCK == 0 and page >= MAX_DISTANCE

    q = matmul(xn, w_in[:, :nq], out_dtype=CDT)
    rows = matmul(xn, w_in[:, nq:nq + 6 * nkv])
    w_gate = jnp.pad(w_in[:, nq + 6 * nkv:], ((0, 0), (0, LANE - N_BRANCH * N_HEADS)))
    gates = matmul(xn, w_gate, act_out="sigmoid")
    rows3 = rows.reshape(N // DEC_T, DEC_T, 6 * nkv)
    q_s = q[T:].astype(F32).reshape(B, DEC_T, nq)

    rows_bf_p = rows[:T].astype(CDT)
    kcvc = compress_prompt(rows[:T], cmp_pe, cmp_w1, cmp_w2)
    oc_p, sel_p = cmp_prompt(q, kcvc, rel_bias, n_q=T)
    tiles_sel = bias_tiles(rel_bias, 3, QB, QB, dstep=QB, keys_first=True)
    os_p = flash_prompt(q, rows_bf_p, 2, rows_bf_p[:, 3 * nkv:4 * nkv].T, tiles_sel, n_q=T, max_back=None, nsub=4,
                        sel_t=sel_p)
    tiles_w = bias_tiles(rel_bias, WIN_A // QB + 1, QB, QB, dstep=QB, window=WIN_A, keys_first=True)
    ow_p = flash_prompt(q, rows_bf_p, 4, rows_bf_p[:, 5 * nkv:6 * nkv].T, tiles_w, n_q=T, max_back=WIN_A // QB)

    pool3 = pool.reshape(n_pool, page * N_SLAB, HEAD_DIM)
    pool_p = pool_compress(page_table, pool3, cmp_pe, cmp_w1)
    n_cmp_s = (past + DEC_T - CMP_LEN) // CMP_STRIDE + 1
    tile_c = bias_tiles(rel_bias, 1, DEC_T, past // CMP_STRIDE, delta0=past, cs=CMP_STRIDE, coff=CMP_LEN - 1,
                        ncols_valid=n_cmp_s)
    oc_s, pen_s = cmp_sample(pool_p, q_s, cmp_w2, tile_c, page=page)
    tiles_past = bias_tiles(rel_bias, 2, DEC_T, page, delta0=2 * page, dstep=-page)
    tile_new = bias_tiles(rel_bias, 1, DEC_T, page)
    os_s = sel_sample(page_table, pool3, q_s, pen_s, rows3, tiles_past, tile_new, g0=n_pg)
    tile_buf = bias_tiles(rel_bias, 1, DEC_T, wb, delta0=wb, window=WIN_A)
    tile_wnew = bias_tiles(rel_bias, 1, DEC_T, LANE, window=WIN_A)
    ow_s, new_win_s = win_sample(q_s, win_state.reshape(B, wb * 2 * KV, HEAD_DIM), rows3, 2, tile_buf, tile_wnew,
                                 g0=n_pg)

    def both(p, s):
        return jnp.concatenate([p, s.reshape(B * DEC_T, nq)], axis=0)

    o = combine(gates, both(oc_p, oc_s), both(os_p, os_s), both(ow_p, ow_s))
    return matmul(o, w_out), rows, new_win_s


def swa_mixer(xn, T, kvsh, w_q, sinks, w_out, rel_bias, state):
    N = xn.shape[0]
    B, wb = state.shape[0], state.shape[1]
    assert wb == WIN_B
    nq, nkv = N_HEADS * HEAD_DIM, KV * HEAD_DIM
    q = matmul(xn, w_q, out_dtype=CDT)
    tiles = bias_tiles(rel_bias, WIN_B // QB + 1, QB, QB, dstep=QB, window=WIN_B, keys_first=True)
    kv_p = kvsh[:T].astype(CDT)
    o_p = flash_prompt(q, kv_p, 0, kv_p[:, nkv:].T, tiles, n_q=T, max_back=WIN_B // QB, sinks=sinks)
    tile_buf = bias_tiles(rel_bias, 1, DEC_T, wb, delta0=wb, window=WIN_B)
    tile_new = bias_tiles(rel_bias, 1, DEC_T, LANE, window=WIN_B)
    o_s, new_state = win_sample(q[T:].astype(F32).reshape(B, DEC_T, nq), state.reshape(B, wb * 2 * KV, HEAD_DIM),
                                kvsh.reshape(N // DEC_T, DEC_T, 2 * nkv), 0, tile_buf, tile_new, g0=T // DEC_T,
                                sinks=sinks)
    o = jnp.concatenate([o_p, o_s.reshape(B * DEC_T, nq)], axis=0).astype(CDT)
    return matmul(o, w_out), new_state


def kernel(x_prompt, x_sample, cache_nsa_kv, state_nsa_win, state_swa_kv, page_table, c_prompt, c_sample,
           ada_w, ada_b, norm_mix, norm_ffn, nsa_w_in, nsa_w_out, nsa_cmp_pe, nsa_cmp_w1, nsa_cmp_w2,
           kv_ada_w, kv_ada_b, kv_norm, kv_w, swa_w_q, swa_sinks, swa_w_out, rel_bias,
           peer_wq, peer_subkeys, peer_u, peer_v, final_norm):
    assert x_prompt.shape[0] == 1 and x_sample.shape[1] == DEC_T
    T, D = x_prompt.shape[1], x_prompt.shape[2]
    B = x_sample.shape[0]
    assert T % QB == 0 and D == N_HEADS * HEAD_DIM
    n_pg = T // DEC_T
    NG = n_pg + B
    N = NG * DEC_T
    nq = N_HEADS * HEAD_DIM
    nkv = KV * HEAD_DIM
    wb_a, wb_b = state_nsa_win.shape[2], state_swa_kv.shape[1]
    assert state_nsa_win.shape[0] == 1 and cache_nsa_kv.shape[0] == 1 and ada_w.shape[0] == 2

    h = jnp.concatenate([x_prompt.reshape(T, D), x_sample.reshape(B * DEC_T, D)], axis=0).reshape(NG, DEC_T, D)
    c = jnp.concatenate([c_prompt, c_sample], axis=0)
    mods = [_group_rows(matmul(c, ada_w, ada_b[i], layer=i, act_in="silu"), n_pg) for i in range(2)]
    mod_kv = _group_rows(matmul(c, kv_ada_w, kv_ada_b, act_in="silu"), n_pg)

    (xn,) = resmod(h, None, [(norm_mix[0], (mods[0], 0, 1))], [CDT], emit_h=False)
    out, rows, new_win_s = nsa_mixer(xn, T, nsa_w_in[0], nsa_w_out[0], nsa_cmp_pe[0], nsa_cmp_w1[0], nsa_cmp_w2[0],
                                     rel_bias, cache_nsa_kv[0], page_table, state_nsa_win[0])
    h, xn = resmod(h, (out.reshape(NG, DEC_T, D), (mods[0], 2)), [(norm_ffn[0], (mods[0], 3, 4))], [CDT], emit_h=True)
    ffn = peer(xn, peer_wq[0], peer_subkeys[0], peer_u, peer_v, 0)

    h, xn, xkv = resmod(h, (ffn.reshape(NG, DEC_T, D), (mods[0], 5)),
                        [(norm_mix[1], (mods[1], 0, 1)), (kv_norm, (mod_kv, 0, 1))], [CDT, CDT], emit_h=True)
    kvsh = matmul(xkv, kv_w)
    out, new_swa_s = swa_mixer(xn, T, kvsh, swa_w_q[0], swa_sinks[0], swa_w_out[0], rel_bias, state_swa_kv)
    h, xn = resmod(h, (out.reshape(NG, DEC_T, D), (mods[1], 2)), [(norm_ffn[1], (mods[1], 3, 4))], [CDT], emit_h=True)
    ffn = peer(xn, peer_wq[1], peer_subkeys[1], peer_u, peer_v, 1)
    (y2d,) = resmod(h, (ffn.reshape(NG, DEC_T, D), (mods[1], 5)), [(final_norm, None)], [F32], emit_h=False)

    rows_p, rows_s = rows[:T], rows[T:]
    return (y2d[:T].reshape(1, T, D),
            y2d[T:].reshape(B, DEC_T, D),
            rows_p[:, :4 * nkv].reshape(1, 1, T, 4, KV, HEAD_DIM),
            rows_s[:, :4 * nkv].reshape(1, B, DEC_T, 4, KV, HEAD_DIM),
            rows_p[T - min(WIN_A, T):, 4 * nkv:].reshape(1, 1, min(WIN_A, T), 2, KV, HEAD_DIM),
            new_win_s.reshape(1, B, wb_a, 2, KV, HEAD_DIM),
            kvsh[T - min(WIN_B, T):T].reshape(1, min(WIN_B, T), 2, KV, HEAD_DIM),
            new_swa_s.reshape(B, wb_b, 2, KV, HEAD_DIM))
```
